```python
import jax, jax.numpy as jnp
from jax import lax
import numpy as np

D_MODEL = 1024
BATCH = 4
SEQ = 4096
DEPTH = 4

GRID_W = 64
CTX_LEN = 256
N_MIXERS = 3
Q_BLOCK = 128
ROPE_THETA = 10000.0
NORM_EPS = 1e-6
N_MOD = 6
MLA_HEADS = 8
MLA_Q_LORA = 512
MLA_KV_LORA = 256
MLA_NOPE = 128
MLA_ROPE = 64
MLA_V = 128
CONV_WIDTH = 3
CONV_DIM = D_MODEL
GQA_HEADS = 16
GQA_KV_HEADS = 8
GQA_HEAD_DIM = 128
MOE_GROUPS = 4
MOE_EXPERTS_PER_GROUP = 8
MOE_EXPERTS = MOE_GROUPS * MOE_EXPERTS_PER_GROUP
MOE_TOP_K = 2
MOE_FF = 512
MOE_BLOCK = 128

kernel_name = "hybrid_mla_conv_gqa_hier_moe_dit"


def _n_layers_of(kind):
    return len(range(kind, DEPTH, N_MIXERS))


def _rmsnorm(x, g):
    xf = x.astype(jnp.float32)
    y = xf * lax.rsqrt(jnp.mean(xf * xf, axis=-1, keepdims=True) + NORM_EPS)
    return (y * g.astype(jnp.float32)).astype(x.dtype)


def _modulate(x, g, shift, scale):
    return _rmsnorm(x, g) * (1 + scale) + shift


def _rope_1d(x, pos):
    half = x.shape[-1] // 2
    inv_freq = ROPE_THETA ** (-jnp.arange(half, dtype=jnp.float32) / half)
    ang = pos.astype(jnp.float32)[:, None] * inv_freq[None, :]
    cos, sin = jnp.cos(ang)[:, None, :], jnp.sin(ang)[:, None, :]
    xf = x.astype(jnp.float32)
    x1, x2 = xf[..., :half], xf[..., half:]
    return jnp.concatenate([x1 * cos - x2 * sin, x1 * sin + x2 * cos], axis=-1).astype(x.dtype)


def _axial_rope(x, rows, cols):
    half = x.shape[-1] // 2
    return jnp.concatenate([_rope_1d(x[..., :half], rows), _rope_1d(x[..., half:], cols)], axis=-1)


def _attend(q, k, v, scale):
    B, n, H, dk = q.shape
    Hk, dv = k.shape[2], v.shape[-1]
    G = H // Hk
    nb = n // Q_BLOCK
    qb = q.reshape(B, nb, Q_BLOCK, Hk, G, dk).transpose(1, 0, 2, 3, 4, 5)

    def block(qblk):
        s = jnp.einsum("bqhgd,bkhd->bhgqk", qblk, k).astype(jnp.float32) * scale
        p = jax.nn.softmax(s, axis=-1).astype(v.dtype)
        return jnp.einsum("bhgqk,bkhd->bqhgd", p, v)

    o = lax.map(block, qb)
    return o.transpose(1, 0, 2, 3, 4, 5).reshape(B, n, H * dv)


def _mla_project(a, w_dq, g_q, w_uq, w_dkv, g_kv, w_ukv, pos, want_q):
    B, n, _ = a.shape
    ckv = a @ w_dkv
    c_kv = _rmsnorm(ckv[..., :MLA_KV_LORA], g_kv)
    k_pe = ckv[..., MLA_KV_LORA:][:, :, None, :]
    kv = (c_kv @ w_ukv).reshape(B, n, MLA_HEADS, MLA_NOPE + MLA_V)
    k_nope, v = kv[..., :MLA_NOPE], kv[..., MLA_NOPE:]
    if pos is not None:
        k_pe = _axial_rope(k_pe, *pos)
    k = jnp.concatenate([k_nope, jnp.broadcast_to(k_pe, (B, n, MLA_HEADS, MLA_ROPE))], axis=-1)
    q = None
    if want_q:
        cq = _rmsnorm(a @ w_dq, g_q)
        q = (cq @ w_uq).reshape(B, n, MLA_HEADS, MLA_NOPE + MLA_ROPE)
        if pos is not None:
            q = jnp.concatenate([q[..., :MLA_NOPE], _axial_rope(q[..., MLA_NOPE:], *pos)], axis=-1)
    return q, k, v


def _mla(a_lat, a_ctx, pos, ctx_out, w_dq, g_q, w_uq, w_dkv, g_kv, w_ukv, w_o):
    scale = (MLA_NOPE + MLA_ROPE) ** -0.5
    q_l, k_l, v_l = _mla_project(a_lat, w_dq, g_q, w_uq, w_dkv, g_kv, w_ukv, pos, True)
    q_c, k_c, v_c = _mla_project(a_ctx, w_dq, g_q, w_uq, w_dkv, g_kv, w_ukv, None, ctx_out)
    k_all = jnp.concatenate([k_c, k_l], axis=1)
    v_all = jnp.concatenate([v_c, v_l], axis=1)
    y_lat = _attend(q_l, k_all, v_all, scale) @ w_o
    y_ctx = _attend(q_c, k_c, v_c, scale) @ w_o if ctx_out else None
    return y_lat, y_ctx


def _short_conv_seq(a, w_in, conv_w, w_out):
    bcu = a @ w_in
    b_gate, c_gate, u = jnp.split(bcu, 3, axis=-1)
    z = lax.conv_general_dilated(
        c_gate * u, conv_w[:, None, :], window_strides=(1,),
        padding=[((CONV_WIDTH - 1) // 2, (CONV_WIDTH - 1) // 2)],
        dimension_numbers=("NWC", "WIO", "NWC"), feature_group_count=CONV_DIM)
    return (b_gate * z) @ w_out


def _short_conv(a_lat, a_ctx, ctx_out, w_in, conv_w, w_out):
    y_lat = _short_conv_seq(a_lat, w_in, conv_w, w_out)
    y_ctx = _short_conv_seq(a_ctx, w_in, conv_w, w_out) if ctx_out else None
    return y_lat, y_ctx


def _gqa_project(a, w_qkv, q_g, k_g, pos, want_q):
    B, n, _ = a.shape
    dq = GQA_HEADS * GQA_HEAD_DIM
    dkv = GQA_KV_HEADS * GQA_HEAD_DIM
    kv = a @ w_qkv[:, dq:]
    k = _rmsnorm(kv[..., :dkv].reshape(B, n, GQA_KV_HEADS, GQA_HEAD_DIM), k_g)
    v = kv[..., dkv:].reshape(B, n, GQA_KV_HEADS, GQA_HEAD_DIM)
    if pos is not None:
        k = _axial_rope(k, *pos)
    q = None
    if want_q:
        q = _rmsnorm((a @ w_qkv[:, :dq]).reshape(B, n, GQA_HEADS, GQA_HEAD_DIM), q_g)
        if pos is not None:
            q = _axial_rope(q, *pos)
    return q, k, v


def _gqa(a_lat, a_ctx, pos, ctx_out, w_qkv, q_g, k_g, w_o):
    scale = GQA_HEAD_DIM ** -0.5
    q_l, k_l, v_l = _gqa_project(a_lat, w_qkv, q_g, k_g, pos, True)
    q_c, k_c, v_c = _gqa_project(a_ctx, w_qkv, q_g, k_g, None, ctx_out)
    k_all = jnp.concatenate([k_c, k_l], axis=1)
    v_all = jnp.concatenate([v_c, v_l], axis=1)
    y_lat = _attend(q_l, k_all, v_all, scale) @ w_o
    y_ctx = _attend(q_c, k_c, v_c, scale) @ w_o if ctx_out else None
    return y_lat, y_ctx


def _hier_moe(h, w_group, b_group, w_expert, b_expert, w_gate, w_up, w_down):
    T, D = h.shape
    hf = h.astype(jnp.float32)
    g_prob = jax.nn.softmax(hf @ w_group.astype(jnp.float32), axis=-1)
    g_sel = jnp.argmax(g_prob + b_group.astype(jnp.float32), axis=-1)
    g_w = jnp.take_along_axis(g_prob, g_sel[:, None], axis=-1)
    e_logits = (hf @ w_expert.astype(jnp.float32)).reshape(T, MOE_GROUPS, MOE_EXPERTS_PER_GROUP)
    e_prob = jax.nn.softmax(jnp.take_along_axis(e_logits, g_sel[:, None, None], axis=1)[:, 0], axis=-1)
    e_bias = b_expert.astype(jnp.float32).reshape(MOE_GROUPS, MOE_EXPERTS_PER_GROUP)[g_sel]
    _, top_idx = lax.top_k(e_prob + e_bias, MOE_TOP_K)
    top_w = jnp.take_along_axis(e_prob, top_idx, axis=-1)
    top_w = top_w / jnp.sum(top_w, axis=-1, keepdims=True) * g_w
    expert_id = g_sel[:, None] * MOE_EXPERTS_PER_GROUP + top_idx

    A = T * MOE_TOP_K
    eid = expert_id.reshape(A)
    tok = jnp.arange(A, dtype=jnp.int32) // MOE_TOP_K
    wts = top_w.reshape(A)
    order = jnp.argsort(eid)
    e_s, tok_s, w_s = eid[order], tok[order], wts[order]
    counts = jnp.bincount(eid, length=MOE_EXPERTS)
    starts = jnp.cumsum(counts) - counts
    padded = (counts + MOE_BLOCK - 1) // MOE_BLOCK * MOE_BLOCK
    pends = jnp.cumsum(padded)
    pstarts = pends - padded
    dest = pstarts[e_s] + (jnp.arange(A, dtype=jnp.int32) - starts[e_s])
    nb = -(-A // MOE_BLOCK) + MOE_EXPERTS
    xp = jnp.zeros((nb * MOE_BLOCK, D), h.dtype).at[dest].set(h[tok_s])
    blk_e = jnp.clip(jnp.searchsorted(pends, jnp.arange(nb, dtype=jnp.int32) * MOE_BLOCK, side="right"),
                     0, MOE_EXPERTS - 1)

    def expert_block(args):
        xb, e = args
        return (jax.nn.silu(xb @ w_gate[e]) * (xb @ w_up[e])) @ w_down[e]

    yp = lax.map(expert_block, (xp.reshape(nb, MOE_BLOCK, D), blk_e)).reshape(nb * MOE_BLOCK, D)
    return jax.ops.segment_sum(yp[dest] * w_s[:, None].astype(h.dtype), tok_s, num_segments=T)


def _normal(k, shape, scale):
    return jax.random.normal(k, shape, jnp.float32) * scale


def setup_inputs(seed: int = 0) -> dict:
    key = jax.random.key(seed)
    ks = iter(jax.random.split(key, 40))
    D = D_MODEL
    nA, nB, nC = _n_layers_of(0), _n_layers_of(1), _n_layers_of(2)
    dq = GQA_HEADS * GQA_HEAD_DIM
    dkv = GQA_KV_HEADS * GQA_HEAD_DIM
    return {
        "x": _normal(next(ks), (BATCH, SEQ, D), 1.0),
        "c": _normal(next(ks), (BATCH, D), 1.0),
        "ctx": _normal(next(ks), (BATCH, CTX_LEN, D), 1.0),
        "c_ctx": _normal(next(ks), (D,), 1.0),
        "w_mod": _normal(next(ks), (DEPTH, D, N_MOD * D), 0.5 * D ** -0.5),
        "b_mod": _normal(next(ks), (DEPTH, N_MOD * D), 0.02),
        "norm_mix_g": 1.0 + _normal(next(ks), (DEPTH, D), 0.05),
        "norm_ffn_g": 1.0 + _normal(next(ks), (DEPTH, D), 0.05),
        "mla_w_dq": _normal(next(ks), (nA, D, MLA_Q_LORA), D ** -0.5),
        "mla_g_q": 1.0 + _normal(next(ks), (nA, MLA_Q_LORA), 0.05),
        "mla_w_uq": _normal(next(ks), (nA, MLA_Q_LORA, MLA_HEADS * (MLA_NOPE + MLA_ROPE)), MLA_Q_LORA ** -0.5),
        "mla_w_dkv": _normal(next(ks), (nA, D, MLA_KV_LORA + MLA_ROPE), D ** -0.5),
        "mla_g_kv": 1.0 + _normal(next(ks), (nA, MLA_KV_LORA), 0.05),
        "mla_w_ukv": _normal(next(ks), (nA, MLA_KV_LORA, MLA_HEADS * (MLA_NOPE + MLA_V)), MLA_KV_LORA ** -0.5),
        "mla_w_o": _normal(next(ks), (nA, MLA_HEADS * MLA_V, D), (MLA_HEADS * MLA_V) ** -0.5),
        "conv_w_in": _normal(next(ks), (nB, D, 3 * CONV_DIM), D ** -0.5),
        "conv_w": _normal(next(ks), (nB, CONV_WIDTH, CONV_DIM), CONV_WIDTH ** -0.5),
        "conv_w_out": _normal(next(ks), (nB, CONV_DIM, D), CONV_DIM ** -0.5),
        "gqa_w_qkv": _normal(next(ks), (nC, D, dq + 2 * dkv), D ** -0.5),
        "gqa_q_norm_g": 1.0 + _normal(next(ks), (nC, GQA_HEAD_DIM), 0.05),
        "gqa_k_norm_g": 1.0 + _normal(next(ks), (nC, GQA_HEAD_DIM), 0.05),
        "gqa_w_o": _normal(next(ks), (nC, dq, D), dq ** -0.5),
        "moe_w_group": _normal(next(ks), (DEPTH, D, MOE_GROUPS), D ** -0.5),
        "moe_b_group": _normal(next(ks), (DEPTH, MOE_GROUPS), 0.01),
        "moe_w_expert": _normal(next(ks), (DEPTH, D, MOE_EXPERTS), D ** -0.5),
        "moe_b_expert": _normal(next(ks), (DEPTH, MOE_EXPERTS), 0.01),
        "moe_w_gate": _normal(next(ks), (DEPTH, MOE_EXPERTS, D, MOE_FF), D ** -0.5),
        "moe_w_up": _normal(next(ks), (DEPTH, MOE_EXPERTS, D, MOE_FF), D ** -0.5),
        "moe_w_down": _normal(next(ks), (DEPTH, MOE_EXPERTS, MOE_FF, D), MOE_FF ** -0.5),
        "final_norm_g": 1.0 + _normal(next(ks), (D,), 0.05),
    }


def reference(x, c, ctx, c_ctx, w_mod, b_mod, norm_mix_g, norm_ffn_g,
              mla_w_dq, mla_g_q, mla_w_uq, mla_w_dkv, mla_g_kv, mla_w_ukv, mla_w_o,
              conv_w_in, conv_w, conv_w_out,
              gqa_w_qkv, gqa_q_norm_g, gqa_k_norm_g, gqa_w_o,
              moe_w_group, moe_b_group, moe_w_expert, moe_b_expert, moe_w_gate, moe_w_up, moe_w_down,
              final_norm_g):
    B, S, D = x.shape
    L = ctx.shape[1]
    ROWS = S // GRID_W
    rows = jnp.repeat(jnp.arange(ROWS, dtype=jnp.int32), GRID_W)
    cols = jnp.tile(jnp.arange(GRID_W, dtype=jnp.int32), ROWS)
    pos = (rows, cols)
    cond_lat = jax.nn.silu(c)
    cond_ctx = jax.nn.silu(c_ctx)[None]
    h = ctx
    for i in range(DEPTH):
        kind, j = i % N_MIXERS, i // N_MIXERS
        ctx_out = i < DEPTH - 1
        mod_l = (cond_lat @ w_mod[i] + b_mod[i])[:, None, :]
        mod_c = (cond_ctx @ w_mod[i] + b_mod[i])[:, None, :]
        sh1, sc1, g1, sh2, sc2, g2 = jnp.split(mod_l, N_MOD, axis=-1)
        csh1, csc1, cg1, csh2, csc2, cg2 = jnp.split(mod_c, N_MOD, axis=-1)
        a_lat = _modulate(x, norm_mix_g[i], sh1, sc1)
        a_ctx = _modulate(h, norm_mix_g[i], csh1, csc1)
        if kind == 0:
            y_lat, y_ctx = _mla(a_lat, a_ctx, pos, ctx_out, mla_w_dq[j], mla_g_q[j], mla_w_uq[j],
                                mla_w_dkv[j], mla_g_kv[j], mla_w_ukv[j], mla_w_o[j])
        elif kind == 1:
            y_lat, y_ctx = _short_conv(a_lat, a_ctx, ctx_out, conv_w_in[j], conv_w[j], conv_w_out[j])
        else:
            y_lat, y_ctx = _gqa(a_lat, a_ctx, pos, ctx_out, gqa_w_qkv[j], gqa_q_norm_g[j],
                                gqa_k_norm_g[j], gqa_w_o[j])
        x = x + g1 * y_lat
        f_lat = _modulate(x, norm_ffn_g[i], sh2, sc2)
        moe_args = (moe_w_group[i], moe_b_group[i], moe_w_expert[i], moe_b_expert[i],
                    moe_w_gate[i], moe_w_up[i], moe_w_down[i])
        if ctx_out:
            h = h + cg1 * y_ctx
            f_ctx = _modulate(h, norm_ffn_g[i], csh2, csc2)
            tokens = jnp.concatenate([f_lat.reshape(B * S, D), f_ctx.reshape(B * L, D)], axis=0)
            y = _hier_moe(tokens, *moe_args)
            x = x + g2 * y[:B * S].reshape(B, S, D)
            h = h + cg2 * y[B * S:].reshape(B, L, D)
        else:
            y = _hier_moe(f_lat.reshape(B * S, D), *moe_args)
            x = x + g2 * y.reshape(B, S, D)
    return _rmsnorm(x, final_norm_g)
```

```python
import functools

import jax
import jax.numpy as jnp
from jax import lax
from jax.experimental import pallas as pl
from jax.experimental.pallas import tpu as pltpu

f32 = jnp.float32
bf16 = jnp.bfloat16

GRID_W = 64
ROPE_THETA = 10000.0
NORM_EPS = 1e-6
N_MOD = 6
N_MIXERS = 3
MLA_HEADS, MLA_Q_LORA, MLA_KV_LORA, MLA_NOPE, MLA_ROPE, MLA_V = 8, 512, 256, 128, 64, 128
GQA_HEADS, GQA_KV_HEADS, GQA_HEAD_DIM = 16, 8, 128
MOE_GROUPS, MOE_EXPERTS_PER_GROUP, MOE_TOP_K = 4, 8, 2
MOE_EXPERTS = MOE_GROUPS * MOE_EXPERTS_PER_GROUP

LANES = 128
TM = 256
EXPERT_BLOCK = 256
HALO = 16
VMEM_LIMIT = 56 * 1024 * 1024
ROUTE_LANE0 = MOE_GROUPS
NEG = -1e30


def _params(*sem):
    return pltpu.CompilerParams(dimension_semantics=sem, vmem_limit_bytes=VMEM_LIMIT)


def _rms(x):
    return x * lax.rsqrt(jnp.mean(x * x, axis=-1, keepdims=True) + NORM_EPS)


def _modulate(x, gn, sh, sc):
    return _rms(x) * gn * (1.0 + sc) + sh


def _silu(x):
    return x / (1.0 + jnp.exp(-x))


def _rope(x, c, s1, s2, shift):
    return x * c + pltpu.roll(x, LANES - shift, 1) * s1 + pltpu.roll(x, shift, 1) * s2


def _mod_body(cond_ref, w_ref, b_ref, o_ref):
    a = _silu(cond_ref[...])
    o_ref[...] = jnp.dot(a, w_ref[...], preferred_element_type=f32,
                         precision=lax.Precision.HIGHEST) + b_ref[...]


def _mod_call(cond8, w_mod, b_mod):
    depth, d, n6 = w_mod.shape
    tn = n6 // 4
    return pl.pallas_call(
        _mod_body,
        grid=(depth, n6 // tn),
        in_specs=[pl.BlockSpec((8, d), lambda l, j: (0, 0)),
                  pl.BlockSpec((None, d, tn), lambda l, j: (l, 0, j)),
                  pl.BlockSpec((None, 1, tn), lambda l, j: (l, 0, j))],
        out_specs=pl.BlockSpec((None, 8, tn), lambda l, j: (l, 0, j)),
        out_shape=jax.ShapeDtypeStruct((depth, 8, n6), f32),
        compiler_params=_params("arbitrary", "arbitrary"),
        name="mod_vectors",
    )(cond8, w_mod, b_mod.reshape(depth, 1, n6))


def _mod_spec(chunk, tiles_per_b, n_batch):
    def imap(i):
        r = jnp.where(i % tiles_per_b == 0, n_batch, i // tiles_per_b)
        return (r * N_MOD + chunk, 0, 0)
    return pl.BlockSpec((None, 1, 1024), imap)


def _row_spec(n):
    return pl.BlockSpec((1, n), lambda i: (0, 0))


def _full_spec(shape):
    return pl.BlockSpec(shape, lambda i: (0,) * len(shape))


def _tile_spec(n):
    return pl.BlockSpec((TM, n), lambda i: (i, 0))


def _mla_proj_body(x_ref, sh_ref, sc_ref, gn_ref, wcat_ref, gq_ref, gkv_ref, wuq_ref, wukv_ref,
                   c_ref, s1_ref, s2_ref, q_ref, k_ref, v_ref, *, scale):
    a = _modulate(x_ref[...], gn_ref[...], sh_ref[...], sc_ref[...]).astype(bf16)
    z = jnp.dot(a, wcat_ref[...], preferred_element_type=f32)
    cq = (_rms(z[:, :MLA_Q_LORA]) * gq_ref[...]).astype(bf16)
    ckv = (_rms(z[:, MLA_Q_LORA:MLA_Q_LORA + MLA_KV_LORA]) * gkv_ref[...]).astype(bf16)
    kpe = z[:, MLA_Q_LORA + MLA_KV_LORA:]
    q = jnp.dot(cq, wuq_ref[...], preferred_element_type=f32)
    kv = jnp.dot(ckv, wukv_ref[...], preferred_element_type=f32)
    c, s1, s2 = c_ref[...], s1_ref[...], s2_ref[...]
    half = MLA_ROPE // 4
    kpe_r = _rope(kpe, c, s1, s2, half).astype(bf16)
    hd = 2 * LANES
    for h in range(MLA_HEADS):
        q_ref[:, h * hd:h * hd + LANES] = (q[:, h * hd:h * hd + LANES] * scale).astype(bf16)
        q_ref[:, h * hd + LANES:(h + 1) * hd] = (
            _rope(q[:, h * hd + LANES:(h + 1) * hd], c, s1, s2, half) * scale).astype(bf16)
        k_ref[:, h * hd:h * hd + LANES] = kv[:, h * LANES:(h + 1) * LANES].astype(bf16)
        k_ref[:, h * hd + LANES:(h + 1) * hd] = kpe_r
    v_ref[...] = kv[:, MLA_HEADS * MLA_NOPE:].astype(bf16)


def _gqa_proj_body(x_ref, sh_ref, sc_ref, gn_ref, w_ref, qg_ref, kg_ref,
                   c_ref, s1_ref, s2_ref, q_ref, k_ref, v_ref, *, scale):
    a = _modulate(x_ref[...], gn_ref[...], sh_ref[...], sc_ref[...]).astype(bf16)
    z = jnp.dot(a, w_ref[...], preferred_element_type=f32)
    c, s1, s2 = c_ref[...], s1_ref[...], s2_ref[...]
    half = GQA_HEAD_DIM // 4
    dq = GQA_HEADS * GQA_HEAD_DIM
    dkv = GQA_KV_HEADS * GQA_HEAD_DIM
    qg = qg_ref[...] * scale
    kg = kg_ref[...]
    for h in range(GQA_HEADS):
        qh = _rms(z[:, h * LANES:(h + 1) * LANES]) * qg
        q_ref[:, h * LANES:(h + 1) * LANES] = _rope(qh, c, s1, s2, half).astype(bf16)
    for h in range(GQA_KV_HEADS):
        kh = _rms(z[:, dq + h * LANES:dq + (h + 1) * LANES]) * kg
        k_ref[:, h * LANES:(h + 1) * LANES] = _rope(kh, c, s1, s2, half).astype(bf16)
    v_ref[...] = z[:, dq + dkv:].astype(bf16)


def _conv_proj_body(x_ref, sh_ref, sc_ref, gn_ref, w_ref, bg_ref, cu_ref):
    a = _modulate(x_ref[...], gn_ref[...], sh_ref[...], sc_ref[...]).astype(bf16)
    z = jnp.dot(a, w_ref[...], preferred_element_type=f32)
    d = bg_ref.shape[-1]
    bg_ref[...] = z[:, :d].astype(bf16)
    cu_ref[...] = (z[:, d:2 * d] * z[:, 2 * d:]).astype(bf16)


def _attn_body(q_ref, k_ref, v_ref, o_ref, *, n_ctx, n_all):
    def run(nk):
        s = lax.dot_general(q_ref[...], k_ref[:nk, :], (((1,), (1,)), ((), ())),
                            preferred_element_type=f32)
        m = jnp.max(s, axis=-1, keepdims=True)
        p = jnp.exp(s - m)
        l = jnp.sum(p, axis=-1, keepdims=True)
        o = jnp.dot(p.astype(bf16), v_ref[:nk, :], preferred_element_type=f32)
        o_ref[...] = (o / l).astype(o_ref.dtype)

    i = pl.program_id(2)
    pl.when(i == 0)(lambda: run(n_ctx))
    pl.when(i > 0)(lambda: run(n_all))


def _attn_call(q, k, v, n_batch, heads, kv_heads, dk, dv, n_ctx):
    nt = q.shape[0] // n_batch
    g = heads // kv_heads
    q3, k3, v3 = (a.reshape(n_batch, nt, a.shape[-1]) for a in (q, k, v))
    out = pl.pallas_call(
        functools.partial(_attn_body, n_ctx=n_ctx, n_all=nt),
        grid=(n_batch, heads, nt // TM),
        in_specs=[pl.BlockSpec((None, TM, dk), lambda b, h, i: (b, i, h)),
                  pl.BlockSpec((None, nt, dk), lambda b, h, i: (b, 0, h // g)),
                  pl.BlockSpec((None, nt, dv), lambda b, h, i: (b, 0, h // g))],
        out_specs=pl.BlockSpec((None, TM, dv), lambda b, h, i: (b, i, h)),
        out_shape=jax.ShapeDtypeStruct((n_batch, nt, heads * dv), bf16),
        compiler_params=_params("arbitrary", "arbitrary", "arbitrary"),
        name="attention",
    )(q3, k3, v3)
    return out.reshape(n_batch * nt, heads * dv)


def _route_epilogue(x, y, g1_ref, gn2_ref, sh2_ref, sc2_ref, wr_ref, br_ref, ltri_ref,
                    xo_ref, f_ref, meta_ref, wts_ref, cnt_ref, carry_ref):
    i = pl.program_id(0)

    @pl.when(i == 0)
    def _():
        carry_ref[...] = jnp.zeros_like(carry_ref)

    xn = x + g1_ref[...] * y
    xo_ref[...] = xn
    f = _modulate(xn, gn2_ref[...], sh2_ref[...], sc2_ref[...])
    f_ref[...] = f
    logits = jnp.dot(f, wr_ref[...], preferred_element_type=f32, precision=lax.Precision.HIGHEST)
    lane = lax.broadcasted_iota(jnp.int32, logits.shape, 1).astype(f32)
    bias = br_ref[...]

    def first_argmax(score):
        top = jnp.max(score, axis=-1, keepdims=True)
        return jnp.min(jnp.where(score == top, lane, float(LANES)), axis=-1, keepdims=True)

    def masked_softmax(mask):
        z = jnp.where(mask, logits, NEG)
        e = jnp.exp(z - jnp.max(z, axis=-1, keepdims=True))
        return e / jnp.sum(e, axis=-1, keepdims=True)

    def pick(onehot, v):
        return jnp.sum(jnp.where(onehot, v, 0.0), axis=-1, keepdims=True)

    gmask = lane < float(MOE_GROUPS)
    gp = masked_softmax(gmask)
    g_sel = first_argmax(jnp.where(gmask, gp + bias, NEG))
    g_w = pick(lane == g_sel, gp)
    lo = float(ROUTE_LANE0) + float(MOE_EXPERTS_PER_GROUP) * g_sel
    emask = (lane >= lo) & (lane < lo + float(MOE_EXPERTS_PER_GROUP))
    ep = masked_softmax(emask)
    escore = jnp.where(emask, ep + bias, NEG)
    i1 = first_argmax(escore)
    oh1 = lane == i1
    i2 = first_argmax(jnp.where(oh1, NEG, escore))
    oh2 = lane == i2
    p1, p2 = pick(oh1, ep), pick(oh2, ep)
    w1 = p1 / (p1 + p2) * g_w
    w2 = p2 / (p1 + p2) * g_w

    hits = jnp.where(oh1 | oh2, 1.0, 0.0)
    before = jnp.dot(ltri_ref[...], hits.astype(bf16), preferred_element_type=f32) + carry_ref[...]
    r1, r2 = pick(oh1, before), pick(oh2, before)
    carry_ref[...] = carry_ref[...] + jnp.sum(hits, axis=0, keepdims=True)
    cnt_ref[...] = carry_ref[...]

    col = lax.broadcasted_iota(jnp.int32, meta_ref.shape, 1)
    e1, e2 = i1 - float(ROUTE_LANE0), i2 - float(ROUTE_LANE0)
    meta = jnp.where(col == 0, e1, jnp.where(col == 1, e2, jnp.where(col == 2, r1, r2)))
    meta_ref[...] = meta.astype(jnp.int32)
    wts_ref[...] = jnp.where(col == 0, w1, w2)


def _attn_out_body(y_ref, wo_ref, x_ref, *rest):
    y = jnp.dot(y_ref[...], wo_ref[...], preferred_element_type=f32)
    _route_epilogue(x_ref[...], y, *rest)


def _conv_out_body(cu_ref, prev_ref, next_ref, bg_ref, cw_ref, wo_ref, x_ref, *rest, tiles_per_b):
    t = pl.program_id(0) % tiles_per_b
    has_prev = (t >= 2).astype(f32)
    has_next = ((t != 0) & (t != tiles_per_b - 1)).astype(f32)
    cu = cu_ref[...].astype(f32)
    row = lax.broadcasted_iota(jnp.int32, cu.shape, 0)
    hp = prev_ref[HALO - 1:HALO, :].astype(f32) * has_prev
    hn = next_ref[0:1, :].astype(f32) * has_next
    up = jnp.where(row == 0, hp, pltpu.roll(cu, 1, 0))
    dn = jnp.where(row == TM - 1, hn, pltpu.roll(cu, TM - 1, 0))
    cw = cw_ref[...]
    z = cw[0:1, :] * up + cw[1:2, :] * cu + cw[2:3, :] * dn
    yb = (bg_ref[...].astype(f32) * z).astype(bf16)
    y = jnp.dot(yb, wo_ref[...], preferred_element_type=f32)
    _route_epilogue(x_ref[...], y, *rest)


def _route_specs(d, tiles_per_b, n_batch):
    ins = [_mod_spec(2, tiles_per_b, n_batch), _row_spec(d), _mod_spec(3, tiles_per_b, n_batch),
           _mod_spec(4, tiles_per_b, n_batch), _full_spec((d, LANES)), _row_spec(LANES),
           _full_spec((TM, TM))]
    outs = [_tile_spec(d), _tile_spec(d), _tile_spec(8), _tile_spec(8), _row_spec(LANES)]
    return ins, outs


def _route_out_shapes(t, d):
    return [jax.ShapeDtypeStruct((t, d), f32), jax.ShapeDtypeStruct((t, d), f32),
            jax.ShapeDtypeStruct((t, 8), jnp.int32), jax.ShapeDtypeStruct((t, 8), f32),
            jax.ShapeDtypeStruct((1, LANES), f32)]


def _dispatch_body(dest_ref, f_ref, xp_in, xp_out, sem):
    del xp_in

    def issue(r, carry):
        for k in range(MOE_TOP_K):
            d = dest_ref[0, 0, k * TM + r]
            pltpu.make_async_copy(f_ref.at[pl.ds(r, 1), :], xp_out.at[pl.ds(d, 1), :], sem).start()
        return carry

    lax.fori_loop(0, TM, issue, 0)
    for k in range(MOE_TOP_K):
        pltpu.make_async_copy(f_ref, xp_out.at[pl.ds(0, TM), :], sem).wait()


def _dispatch_call(dest, f, n_rows):
    t, d = f.shape
    xp0 = jnp.zeros((n_rows, d), f32)
    return pl.pallas_call(
        _dispatch_body,
        grid=(t // TM,),
        in_specs=[pl.BlockSpec((1, 1, MOE_TOP_K * TM), lambda i: (i, 0, 0), memory_space=pltpu.SMEM),
                  _tile_spec(d),
                  pl.BlockSpec(memory_space=pl.ANY)],
        out_specs=pl.BlockSpec(memory_space=pl.ANY),
        out_shape=jax.ShapeDtypeStruct((n_rows, d), f32),
        scratch_shapes=[pltpu.SemaphoreType.DMA(())],
        input_output_aliases={2: 0},
        compiler_params=_params("arbitrary"),
        name="moe_dispatch",
    )(dest, f, xp0)


def _expert_body(be_ref, na_ref, x_ref, wg_ref, wu_ref, wd_ref, o_ref, wg_s, wu_s, wd_s):
    i = pl.program_id(0)
    active = i < na_ref[0]
    changed = (i == 0) | (be_ref[i] != be_ref[jnp.maximum(i - 1, 0)])

    @pl.when(active & changed)
    def _():
        wg_s[...] = wg_ref[...].astype(bf16)
        wu_s[...] = wu_ref[...].astype(bf16)
        wd_s[...] = wd_ref[...].astype(bf16)

    @pl.when(active)
    def _():
        x = x_ref[...].astype(bf16)
        g = jnp.dot(x, wg_s[...], preferred_element_type=f32)
        u = jnp.dot(x, wu_s[...], preferred_element_type=f32)
        h = (_silu(g) * u).astype(bf16)
        o_ref[...] = jnp.dot(h, wd_s[...], preferred_element_type=f32)

    @pl.when(jnp.logical_not(active))
    def _():
        o_ref[...] = jnp.zeros_like(o_ref)


def _expert_call(blk_e, n_act, xp, w_gate, w_up, w_down):
    n_rows, d = xp.shape
    ff = w_gate.shape[-1]

    def blk(i, be, na):
        return jnp.minimum(i, na[0] - 1)

    grid_spec = pltpu.PrefetchScalarGridSpec(
        num_scalar_prefetch=2,
        grid=(n_rows // EXPERT_BLOCK,),
        in_specs=[pl.BlockSpec((EXPERT_BLOCK, d), lambda i, be, na: (blk(i, be, na), 0)),
                  pl.BlockSpec((None, d, ff), lambda i, be, na: (be[blk(i, be, na)], 0, 0)),
                  pl.BlockSpec((None, d, ff), lambda i, be, na: (be[blk(i, be, na)], 0, 0)),
                  pl.BlockSpec((None, ff, d), lambda i, be, na: (be[blk(i, be, na)], 0, 0))],
        out_specs=pl.BlockSpec((EXPERT_BLOCK, d), lambda i, be, na: (i, 0)),
        scratch_shapes=[pltpu.VMEM((d, ff), bf16), pltpu.VMEM((d, ff), bf16), pltpu.VMEM((ff, d), bf16)],
    )
    return pl.pallas_call(
        _expert_body,
        grid_spec=grid_spec,
        out_shape=jax.ShapeDtypeStruct((n_rows, d), f32),
        compiler_params=_params("arbitrary"),
        name="moe_experts",
    )(blk_e, n_act, xp, w_gate, w_up, w_down)


def _combine_body(dest_ref, x_ref, wts_ref, g2_ref, fg_ref, yp_ref, o_ref, gbuf, sem, *, final):
    def issue(r, carry):
        for k in range(MOE_TOP_K):
            d = dest_ref[0, 0, k * TM + r]
            pltpu.make_async_copy(yp_ref.at[pl.ds(d, 1), :], gbuf.at[k, pl.ds(r, 1), :], sem).start()
        return carry

    lax.fori_loop(0, TM, issue, 0)
    for k in range(MOE_TOP_K):
        pltpu.make_async_copy(yp_ref.at[pl.ds(0, TM), :], gbuf.at[k], sem).wait()
    w = wts_ref[...]
    y = w[:, 0:1] * gbuf[0] + w[:, 1:2] * gbuf[1]
    xn = x_ref[...] + g2_ref[...] * y
    if final:
        xn = _rms(xn) * fg_ref[...]
    o_ref[...] = xn


def _combine_call(dest, x, wts, mods3, final_g, yp, n_batch, tiles_per_b, final):
    t, d = x.shape
    t0 = 1 if final else 0
    nt = tiles_per_b - t0

    def tile(b, j):
        return b * tiles_per_b + t0 + j

    def g2_map(b, j):
        r = jnp.where(t0 + j == 0, n_batch, b)
        return (r * N_MOD + 5, 0, 0)

    return pl.pallas_call(
        functools.partial(_combine_body, final=final),
        grid=(n_batch, nt),
        in_specs=[pl.BlockSpec((1, 1, MOE_TOP_K * TM), lambda b, j: (tile(b, j), 0, 0),
                               memory_space=pltpu.SMEM),
                  pl.BlockSpec((TM, d), lambda b, j: (tile(b, j), 0)),
                  pl.BlockSpec((TM, 8), lambda b, j: (tile(b, j), 0)),
                  pl.BlockSpec((None, 1, d), g2_map),
                  pl.BlockSpec((1, d), lambda b, j: (0, 0)),
                  pl.BlockSpec(memory_space=pl.ANY)],
        out_specs=pl.BlockSpec((TM, d), lambda b, j: (b * nt + j, 0)),
        out_shape=jax.ShapeDtypeStruct((n_batch * nt * TM, d), f32),
        scratch_shapes=[pltpu.VMEM((MOE_TOP_K, TM, d), f32), pltpu.SemaphoreType.DMA(())],
        compiler_params=_params("arbitrary", "arbitrary"),
        name="moe_combine",
    )(dest, x, wts, mods3, final_g, yp)


def _rope_tables(seq, n_ctx, half):
    inv = ROPE_THETA ** (-jnp.arange(half, dtype=f32) / half)
    t = jnp.arange(seq, dtype=jnp.int32)
    ang_r = (t // GRID_W).astype(f32)[:, None] * inv[None, :]
    ang_c = (t % GRID_W).astype(f32)[:, None] * inv[None, :]
    zero = jnp.zeros_like(ang_r)
    cos = jnp.concatenate([jnp.cos(ang_r), jnp.cos(ang_r), jnp.cos(ang_c), jnp.cos(ang_c)], axis=-1)
    s1 = jnp.concatenate([-jnp.sin(ang_r), zero, -jnp.sin(ang_c), zero], axis=-1)
    s2 = jnp.concatenate([zero, jnp.sin(ang_r), zero, jnp.sin(ang_c)], axis=-1)
    w = 4 * half
    ctx_c = jnp.ones((n_ctx, w), f32)
    ctx_s = jnp.zeros((n_ctx, w), f32)
    pad = ((0, 0), (0, LANES - w))
    return (jnp.pad(jnp.concatenate([ctx_c, cos], axis=0), pad),
            jnp.pad(jnp.concatenate([ctx_s, s1], axis=0), pad),
            jnp.pad(jnp.concatenate([ctx_s, s2], axis=0), pad))


def _table_spec(tiles_per_b):
    return pl.BlockSpec((TM, LANES), lambda i: (i % tiles_per_b, 0))


def kernel(x, c, ctx, c_ctx, w_mod, b_mod, norm_mix_g, norm_ffn_g, mla_w_dq, mla_g_q, mla_w_uq, mla_w_dkv, mla_g_kv, mla_w_ukv, mla_w_o, conv_w_in, conv_w, conv_w_out, gqa_w_qkv, gqa_q_norm_g, gqa_k_norm_g, gqa_w_o, moe_w_group, moe_b_group, moe_w_expert, moe_b_expert, moe_w_gate, moe_w_up, moe_w_down, final_norm_g):
    n_batch, seq, d = x.shape
    n_ctx = ctx.shape[1]
    depth = w_mod.shape[0]
    assert n_ctx == TM and seq % TM == 0 and d == 1024
    nt = n_ctx + seq
    tiles_per_b = nt // TM
    t_all = n_batch * nt
    n_tiles = t_all // TM
    n_assign = t_all * MOE_TOP_K
    n_blocks = -(-n_assign // EXPERT_BLOCK) + MOE_EXPERTS
    n_rows = n_blocks * EXPERT_BLOCK

    xs = jnp.concatenate([ctx, x], axis=1).reshape(t_all, d)
    cond8 = jnp.concatenate([c, c_ctx[None], jnp.zeros((8 - n_batch - 1, d), f32)], axis=0)
    mods = _mod_call(cond8, w_mod, b_mod)
    mods = mods.reshape(depth, 8 * N_MOD, 1, d)

    mla_tabs = _rope_tables(seq, n_ctx, MLA_ROPE // 4)
    gqa_tabs = _rope_tables(seq, n_ctx, GQA_HEAD_DIM // 4)
    ltri = jnp.tril(jnp.ones((TM, TM), f32), -1).astype(bf16)
    mspec = functools.partial(_mod_spec, tiles_per_b=tiles_per_b, n_batch=n_batch)
    tab_specs = [_table_spec(tiles_per_b)] * 3
    route_in, route_out = _route_specs(d, tiles_per_b, n_batch)
    route_shapes = _route_out_shapes(t_all, d)
    route_scratch = [pltpu.VMEM((1, LANES), f32)]

    for i in range(depth):
        kind, j = i % N_MIXERS, i // N_MIXERS
        m3 = mods[i]
        gmix = norm_mix_g[i][None]
        head_in = [_tile_spec(d), mspec(0), mspec(1), _row_spec(d)]
        wr = jnp.pad(jnp.concatenate([moe_w_group[i], moe_w_expert[i]], axis=1),
                     ((0, 0), (0, LANES - MOE_GROUPS - MOE_EXPERTS)))
        br = jnp.pad(jnp.concatenate([moe_b_group[i], moe_b_expert[i]]),
                     (0, LANES - MOE_GROUPS - MOE_EXPERTS))[None]
        route_args = (m3, norm_ffn_g[i][None], m3, m3, wr, br, ltri)

        if kind == 0:
            dk, dv = 2 * LANES, MLA_V
            wcat = jnp.pad(jnp.concatenate([mla_w_dq[j], mla_w_dkv[j]], axis=1),
                           ((0, 0), (0, LANES - MLA_ROPE))).astype(bf16)
            wuq = jnp.pad(mla_w_uq[j].reshape(MLA_Q_LORA, MLA_HEADS, MLA_NOPE + MLA_ROPE),
                          ((0, 0), (0, 0), (0, dk - MLA_NOPE - MLA_ROPE))
                          ).reshape(MLA_Q_LORA, MLA_HEADS * dk).astype(bf16)
            wukv3 = mla_w_ukv[j].reshape(MLA_KV_LORA, MLA_HEADS, MLA_NOPE + MLA_V)
            wukv = jnp.concatenate([wukv3[..., :MLA_NOPE].reshape(MLA_KV_LORA, -1),
                                    wukv3[..., MLA_NOPE:].reshape(MLA_KV_LORA, -1)], axis=1).astype(bf16)
            q, k, v = pl.pallas_call(
                functools.partial(_mla_proj_body, scale=(MLA_NOPE + MLA_ROPE) ** -0.5),
                grid=(n_tiles,),
                in_specs=head_in + [_full_spec(wcat.shape), _row_spec(MLA_Q_LORA), _row_spec(MLA_KV_LORA),
                                    _full_spec(wuq.shape), _full_spec(wukv.shape)] + tab_specs,
                out_specs=[_tile_spec(MLA_HEADS * dk), _tile_spec(MLA_HEADS * dk), _tile_spec(MLA_HEADS * dv)],
                out_shape=[jax.ShapeDtypeStruct((t_all, MLA_HEADS * dk), bf16),
                           jax.ShapeDtypeStruct((t_all, MLA_HEADS * dk), bf16),
                           jax.ShapeDtypeStruct((t_all, MLA_HEADS * dv), bf16)],
                compiler_params=_params("arbitrary"),
                name="mla_proj",
            )(xs, m3, m3, gmix, wcat, mla_g_q[j][None], mla_g_kv[j][None], wuq, wukv, *mla_tabs)
            att = _attn_call(q, k, v, n_batch, MLA_HEADS, MLA_HEADS, dk, dv, n_ctx)
            w_o = mla_w_o[j].astype(bf16)
        elif kind == 2:
            dk = dv = GQA_HEAD_DIM
            wqkv = gqa_w_qkv[j].astype(bf16)
            q, k, v = pl.pallas_call(
                functools.partial(_gqa_proj_body, scale=GQA_HEAD_DIM ** -0.5),
                grid=(n_tiles,),
                in_specs=head_in + [_full_spec(wqkv.shape), _row_spec(dk), _row_spec(dk)] + tab_specs,
                out_specs=[_tile_spec(GQA_HEADS * dk), _tile_spec(GQA_KV_HEADS * dk),
                           _tile_spec(GQA_KV_HEADS * dv)],
                out_shape=[jax.ShapeDtypeStruct((t_all, GQA_HEADS * dk), bf16),
                           jax.ShapeDtypeStruct((t_all, GQA_KV_HEADS * dk), bf16),
                           jax.ShapeDtypeStruct((t_all, GQA_KV_HEADS * dv), bf16)],
                compiler_params=_params("arbitrary"),
                name="gqa_proj",
            )(xs, m3, m3, gmix, wqkv, gqa_q_norm_g[j][None], gqa_k_norm_g[j][None], *gqa_tabs)
            att = _attn_call(q, k, v, n_batch, GQA_HEADS, GQA_KV_HEADS, dk, dv, n_ctx)
            w_o = gqa_w_o[j].astype(bf16)

        if kind == 1:
            w_in = conv_w_in[j].astype(bf16)
            bg, cu = pl.pallas_call(
                _conv_proj_body,
                grid=(n_tiles,),
                in_specs=head_in + [_full_spec(w_in.shape)],
                out_specs=[_tile_spec(d), _tile_spec(d)],
                out_shape=[jax.ShapeDtypeStruct((t_all, d), bf16)] * 2,
                compiler_params=_params("arbitrary"),
                name="conv_proj",
            )(xs, m3, m3, gmix, w_in)
            w_o = conv_w_out[j].astype(bf16)
            per = TM // HALO
            last = t_all // HALO - 1
            xs, f, meta, wts, cnt = pl.pallas_call(
                functools.partial(_conv_out_body, tiles_per_b=tiles_per_b),
                grid=(n_tiles,),
                in_specs=[_tile_spec(d),
                          pl.BlockSpec((HALO, d), lambda i: (jnp.maximum(i * per - 1, 0), 0)),
                          pl.BlockSpec((HALO, d), lambda i: (jnp.minimum((i + 1) * per, last), 0)),
                          _tile_spec(d), _full_spec((3, d)), _full_spec(w_o.shape), _tile_spec(d)] + route_in,
                out_specs=route_out,
                out_shape=route_shapes,
                scratch_shapes=route_scratch,
                compiler_params=_params("arbitrary"),
                name="conv_out_route",
            )(cu, cu, cu, bg, conv_w[j], w_o, xs, *route_args)
        else:
            xs, f, meta, wts, cnt = pl.pallas_call(
                _attn_out_body,
                grid=(n_tiles,),
                in_specs=[_tile_spec(att.shape[-1]), _full_spec(w_o.shape), _tile_spec(d)] + route_in,
                out_specs=route_out,
                out_shape=route_shapes,
                scratch_shapes=route_scratch,
                compiler_params=_params("arbitrary"),
                name="attn_out_route",
            )(att, w_o, xs, *route_args)

        counts = cnt[0, ROUTE_LANE0:ROUTE_LANE0 + MOE_EXPERTS].astype(jnp.int32)
        padded = (counts + EXPERT_BLOCK - 1) // EXPERT_BLOCK * EXPERT_BLOCK
        pends = jnp.cumsum(padded)
        pstarts = pends - padded
        dest = pstarts[meta[:, :MOE_TOP_K]] + meta[:, MOE_TOP_K:2 * MOE_TOP_K]
        dest = dest.reshape(n_tiles, TM, MOE_TOP_K).transpose(0, 2, 1).reshape(n_tiles, 1, MOE_TOP_K * TM)
        blk_row0 = jnp.arange(n_blocks, dtype=jnp.int32) * EXPERT_BLOCK
        blk_e = jnp.minimum(jnp.sum((pends[None, :] <= blk_row0[:, None]).astype(jnp.int32), axis=1),
                            MOE_EXPERTS - 1)
        n_act = (pends[-1:] // EXPERT_BLOCK).astype(jnp.int32)

        xp = _dispatch_call(dest, f, n_rows)
        yp = _expert_call(blk_e, n_act, xp, moe_w_gate[i], moe_w_up[i], moe_w_down[i])
        final = i == depth - 1
        xs = _combine_call(dest, xs, wts, m3, final_norm_g[None], yp, n_batch, tiles_per_b, final)

    return xs.reshape(n_batch, seq, d)
```

```python
import functools

import jax
import jax.numpy as jnp
from jax import lax
from jax.experimental import pallas as pl
from jax.experimental.pallas import tpu as pltpu

f32 = jnp.float32
bf16 = jnp.bfloat16

GRID_W = 64
ROPE_THETA = 10000.0
NORM_EPS = 1e-6
N_MOD = 6
N_MIXERS = 3
MLA_HEADS, MLA_Q_LORA, MLA_KV_LORA, MLA_NOPE, MLA_ROPE, MLA_V = 8, 512, 256, 128, 64, 128
GQA_HEADS, GQA_KV_HEADS, GQA_HEAD_DIM = 16, 8, 128
MOE_GROUPS, MOE_EXPERTS_PER_GROUP, MOE_TOP_K = 4, 8, 2
MOE_EXPERTS = MOE_GROUPS * MOE_EXPERTS_PER_GROUP

LANES = 128
TM = 256
EXPERT_BLOCK = 256
SUBLANES = 8
PAD_PIECES = tuple(1 << b for b in reversed(range((EXPERT_BLOCK // SUBLANES).bit_length() - 1)))
HALO = 16
ATTN_KEY_CHUNK = 256
ATTN_HEADS_PER_STEP = 4
LOG2E = 1.4426950408889634
VMEM_LIMIT = 56 * 1024 * 1024
ROUTE_LANE0 = MOE_GROUPS
META_ROWS = 8
NEG = -1e30


def _params(*sem):
    return pltpu.CompilerParams(dimension_semantics=sem, vmem_limit_bytes=VMEM_LIMIT)


def _rms(x):
    return x * lax.rsqrt(jnp.mean(x * x, axis=-1, keepdims=True) + NORM_EPS)


def _modulate(x, gn, sh, sc):
    return _rms(x) * gn * (1.0 + sc) + sh


def _silu(x):
    return x / (1.0 + jnp.exp(-x))


def _rope(x, c, s1, s2, shift):
    return x * c + pltpu.roll(x, LANES - shift, 1) * s1 + pltpu.roll(x, shift, 1) * s2


def _mod_body(cond_ref, w_ref, b_ref, o_ref):
    a = _silu(cond_ref[...])
    o_ref[...] = jnp.dot(a, w_ref[...], preferred_element_type=f32,
                         precision=lax.Precision.HIGHEST) + b_ref[...]


def _mod_call(cond8, w_mod, b_mod):
    depth, d, n6 = w_mod.shape
    tn = n6 // 4
    return pl.pallas_call(
        _mod_body,
        grid=(depth, n6 // tn),
        in_specs=[pl.BlockSpec((8, d), lambda l, j: (0, 0)),
                  pl.BlockSpec((None, d, tn), lambda l, j: (l, 0, j)),
                  pl.BlockSpec((None, 1, tn), lambda l, j: (l, 0, j))],
        out_specs=pl.BlockSpec((None, 8, tn), lambda l, j: (l, 0, j)),
        out_shape=jax.ShapeDtypeStruct((depth, 8, n6), f32),
        compiler_params=_params("arbitrary", "arbitrary"),
        name="mod_vectors",
    )(cond8, w_mod, b_mod.reshape(depth, 1, n6))


def _mod_spec(chunk, tiles_per_b, n_batch):
    def imap(i):
        r = jnp.where(i % tiles_per_b == 0, n_batch, i // tiles_per_b)
        return (r * N_MOD + chunk, 0, 0)
    return pl.BlockSpec((None, 1, 1024), imap)


def _row_spec(n):
    return pl.BlockSpec((1, n), lambda i: (0, 0))


def _full_spec(shape):
    return pl.BlockSpec(shape, lambda i: (0,) * len(shape))


def _tile_spec(n):
    return pl.BlockSpec((TM, n), lambda i: (i, 0))


def _mla_proj_body(x_ref, sh_ref, sc_ref, gn_ref, wcat_ref, gq_ref, gkv_ref, wuq_ref, wukv_ref,
                   c_ref, s1_ref, s2_ref, q_ref, k_ref, v_ref, *, scale):
    a = _modulate(x_ref[...], gn_ref[...], sh_ref[...], sc_ref[...]).astype(bf16)
    z = jnp.dot(a, wcat_ref[...], preferred_element_type=f32)
    cq = (_rms(z[:, :MLA_Q_LORA]) * gq_ref[...]).astype(bf16)
    ckv = (_rms(z[:, MLA_Q_LORA:MLA_Q_LORA + MLA_KV_LORA]) * gkv_ref[...]).astype(bf16)
    kpe = z[:, MLA_Q_LORA + MLA_KV_LORA:]
    q = jnp.dot(cq, wuq_ref[...], preferred_element_type=f32)
    kv = jnp.dot(ckv, wukv_ref[...], preferred_element_type=f32)
    c, s1, s2 = c_ref[...], s1_ref[...], s2_ref[...]
    half = MLA_ROPE // 4
    kpe_r = _rope(kpe, c, s1, s2, half).astype(bf16)
    hd = 2 * LANES
    for h in range(MLA_HEADS):
        q_ref[:, h * hd:h * hd + LANES] = (q[:, h * hd:h * hd + LANES] * scale).astype(bf16)
        q_ref[:, h * hd + LANES:(h + 1) * hd] = (
            _rope(q[:, h * hd + LANES:(h + 1) * hd], c, s1, s2, half) * scale).astype(bf16)
        k_ref[:, h * hd:h * hd + LANES] = kv[:, h * LANES:(h + 1) * LANES].astype(bf16)
        k_ref[:, h * hd + LANES:(h + 1) * hd] = kpe_r
    v_ref[...] = kv[:, MLA_HEADS * MLA_NOPE:].astype(bf16)


def _gqa_proj_body(x_ref, sh_ref, sc_ref, gn_ref, w_ref, qg_ref, kg_ref,
                   c_ref, s1_ref, s2_ref, q_ref, k_ref, v_ref, *, scale):
    a = _modulate(x_ref[...], gn_ref[...], sh_ref[...], sc_ref[...]).astype(bf16)
    z = jnp.dot(a, w_ref[...], preferred_element_type=f32)
    c, s1, s2 = c_ref[...], s1_ref[...], s2_ref[...]
    half = GQA_HEAD_DIM // 4
    dq = GQA_HEADS * GQA_HEAD_DIM
    dkv = GQA_KV_HEADS * GQA_HEAD_DIM
    qg = qg_ref[...] * scale
    kg = kg_ref[...]
    for h in range(GQA_HEADS):
        qh = _rms(z[:, h * LANES:(h + 1) * LANES]) * qg
        q_ref[:, h * LANES:(h + 1) * LANES] = _rope(qh, c, s1, s2, half).astype(bf16)
    for h in range(GQA_KV_HEADS):
        kh = _rms(z[:, dq + h * LANES:dq + (h + 1) * LANES]) * kg
        k_ref[:, h * LANES:(h + 1) * LANES] = _rope(kh, c, s1, s2, half).astype(bf16)
    v_ref[...] = z[:, dq + dkv:].astype(bf16)


def _conv_proj_body(x_ref, sh_ref, sc_ref, gn_ref, w_ref, bg_ref, cu_ref):
    a = _modulate(x_ref[...], gn_ref[...], sh_ref[...], sc_ref[...]).astype(bf16)
    z = jnp.dot(a, w_ref[...], preferred_element_type=f32)
    d = bg_ref.shape[-1]
    bg_ref[...] = z[:, :d].astype(bf16)
    cu_ref[...] = (z[:, d:2 * d] * z[:, 2 * d:]).astype(bf16)


def _lane_groups(x, op):
    parts = [x[:, i * LANES:(i + 1) * LANES] for i in range(x.shape[1] // LANES)]
    while len(parts) > 1:
        parts = [op(parts[i], parts[i + 1]) if i + 1 < len(parts) else parts[i]
                 for i in range(0, len(parts), 2)]
    return parts[0]


def _attn_body(q_ref, k_ref, v_ref, o_ref, s_scr, *, n_ctx, n_all, heads_per_step, group, dk, dv):
    def run(nk):
        chunks = [(a, min(a + ATTN_KEY_CHUNK, nk)) for a in range(0, nk, ATTN_KEY_CHUNK)]
        for j in range(heads_per_step):
            kv = j // group
            q = q_ref[:, j * dk:(j + 1) * dk]
            m = None
            for a, b in chunks:
                s = lax.dot_general(q, k_ref[a:b, kv * dk:(kv + 1) * dk], (((1,), (1,)), ((), ())),
                                    preferred_element_type=f32)
                s_scr[j, :, a:b] = s
                mc = _lane_groups(s, jnp.maximum)
                m = mc if m is None else jnp.maximum(m, mc)
            m = jnp.max(m, axis=-1, keepdims=True)
            o = l = None
            for a, b in chunks:
                p = jnp.exp2(s_scr[j, :, a:b] - m)
                lc = _lane_groups(p, jnp.add)
                oc = jnp.dot(p.astype(bf16), v_ref[a:b, kv * dv:(kv + 1) * dv], preferred_element_type=f32)
                o, l = (oc, lc) if o is None else (o + oc, l + lc)
            l = jnp.sum(l, axis=-1, keepdims=True)
            o_ref[:, j * dv:(j + 1) * dv] = (o / l).astype(o_ref.dtype)

    i = pl.program_id(2)
    pl.when(i == 0)(lambda: run(n_ctx))
    pl.when(i > 0)(lambda: run(n_all))


def _attn_call(q, k, v, n_batch, heads, kv_heads, dk, dv, n_ctx):
    nt = q.shape[0] // n_batch
    g = heads // kv_heads
    hp = ATTN_HEADS_PER_STEP
    kh = hp // g
    q3, k3, v3 = (a.reshape(n_batch, nt, a.shape[-1]) for a in (q, k, v))
    out = pl.pallas_call(
        functools.partial(_attn_body, n_ctx=n_ctx, n_all=nt, heads_per_step=hp, group=g, dk=dk, dv=dv),
        grid=(n_batch, heads // hp, nt // TM),
        in_specs=[pl.BlockSpec((None, TM, hp * dk), lambda b, h, i: (b, i, h)),
                  pl.BlockSpec((None, nt, kh * dk), lambda b, h, i: (b, 0, h)),
                  pl.BlockSpec((None, nt, kh * dv), lambda b, h, i: (b, 0, h))],
        out_specs=pl.BlockSpec((None, TM, hp * dv), lambda b, h, i: (b, i, h)),
        out_shape=jax.ShapeDtypeStruct((n_batch, nt, heads * dv), bf16),
        scratch_shapes=[pltpu.VMEM((hp, TM, nt), f32)],
        compiler_params=_params("arbitrary", "arbitrary", "arbitrary"),
        name="attention",
    )(q3, k3, v3)
    return out.reshape(n_batch * nt, heads * dv)


def _route_epilogue(x, y, g1_ref, gn2_ref, sh2_ref, sc2_ref, wr_ref, br_ref, ltri_ref,
                    xo_ref, f_ref, meta_ref, wts_ref, cnt_ref, carry_ref):
    i = pl.program_id(0)

    @pl.when(i == 0)
    def _():
        carry_ref[...] = jnp.zeros_like(carry_ref)

    xn = x + g1_ref[...] * y
    xo_ref[...] = xn
    f = _modulate(xn, gn2_ref[...], sh2_ref[...], sc2_ref[...])
    f_ref[...] = f
    logits = jnp.dot(f, wr_ref[...], preferred_element_type=f32, precision=lax.Precision.HIGHEST)
    lane = lax.broadcasted_iota(jnp.int32, logits.shape, 1).astype(f32)
    bias = br_ref[...]

    def first_argmax(score):
        top = jnp.max(score, axis=-1, keepdims=True)
        return jnp.min(jnp.where(score == top, lane, float(LANES)), axis=-1, keepdims=True)

    def masked_softmax(mask):
        z = jnp.where(mask, logits, NEG)
        e = jnp.exp(z - jnp.max(z, axis=-1, keepdims=True))
        return e / jnp.sum(e, axis=-1, keepdims=True)

    def pick(onehot, v):
        return jnp.sum(jnp.where(onehot, v, 0.0), axis=-1, keepdims=True)

    gmask = lane < float(MOE_GROUPS)
    gp = masked_softmax(gmask)
    g_sel = first_argmax(jnp.where(gmask, gp + bias, NEG))
    g_w = pick(lane == g_sel, gp)
    lo = float(ROUTE_LANE0) + float(MOE_EXPERTS_PER_GROUP) * g_sel
    emask = (lane >= lo) & (lane < lo + float(MOE_EXPERTS_PER_GROUP))
    ep = masked_softmax(emask)
    escore = jnp.where(emask, ep + bias, NEG)
    i1 = first_argmax(escore)
    oh1 = lane == i1
    i2 = first_argmax(jnp.where(oh1, NEG, escore))
    oh2 = lane == i2
    p1, p2 = pick(oh1, ep), pick(oh2, ep)
    w1 = p1 / (p1 + p2) * g_w
    w2 = p2 / (p1 + p2) * g_w

    hits = jnp.where(oh1 | oh2, 1.0, 0.0)
    before = jnp.dot(ltri_ref[...], hits.astype(bf16), preferred_element_type=f32) + carry_ref[...]
    r1, r2 = pick(oh1, before), pick(oh2, before)
    carry_ref[...] = carry_ref[...] + jnp.sum(hits, axis=0, keepdims=True)
    cnt_ref[...] = carry_ref[...]

    e1, e2 = i1 - float(ROUTE_LANE0), i2 - float(ROUTE_LANE0)
    meta = jnp.where(lane == 0.0, e1, jnp.where(lane == 1.0, e2, jnp.where(lane == 2.0, r1, r2)))
    meta_ref[...] = meta.T[:meta_ref.shape[0], :].astype(jnp.int32)
    col = lax.broadcasted_iota(jnp.int32, wts_ref.shape, 1)
    wts_ref[...] = jnp.where(col == 0, w1, w2)


def _attn_out_body(y_ref, wo_ref, x_ref, *rest):
    y = jnp.dot(y_ref[...], wo_ref[...], preferred_element_type=f32)
    _route_epilogue(x_ref[...], y, *rest)


def _conv_out_body(cu_ref, prev_ref, next_ref, bg_ref, cw_ref, wo_ref, x_ref, *rest, tiles_per_b):
    t = pl.program_id(0) % tiles_per_b
    has_prev = (t >= 2).astype(f32)
    has_next = ((t != 0) & (t != tiles_per_b - 1)).astype(f32)
    cu = cu_ref[...].astype(f32)
    row = lax.broadcasted_iota(jnp.int32, cu.shape, 0)
    hp = prev_ref[HALO - 1:HALO, :].astype(f32) * has_prev
    hn = next_ref[0:1, :].astype(f32) * has_next
    up = jnp.where(row == 0, hp, pltpu.roll(cu, 1, 0))
    dn = jnp.where(row == TM - 1, hn, pltpu.roll(cu, TM - 1, 0))
    cw = cw_ref[...]
    z = cw[0:1, :] * up + cw[1:2, :] * cu + cw[2:3, :] * dn
    yb = (bg_ref[...].astype(f32) * z).astype(bf16)
    y = jnp.dot(yb, wo_ref[...], preferred_element_type=f32)
    _route_epilogue(x_ref[...], y, *rest)


def _route_specs(d, tiles_per_b, n_batch):
    ins = [_mod_spec(2, tiles_per_b, n_batch), _row_spec(d), _mod_spec(3, tiles_per_b, n_batch),
           _mod_spec(4, tiles_per_b, n_batch), _full_spec((d, LANES)), _row_spec(LANES),
           _full_spec((TM, TM))]
    outs = [_tile_spec(d), _tile_spec(d), pl.BlockSpec((None, META_ROWS, TM), lambda i: (i, 0, 0)),
            _tile_spec(8), _row_spec(LANES)]
    return ins, outs


def _route_out_shapes(t, d):
    return [jax.ShapeDtypeStruct((t, d), f32), jax.ShapeDtypeStruct((t, d), f32),
            jax.ShapeDtypeStruct((t // TM, META_ROWS, TM), jnp.int32), jax.ShapeDtypeStruct((t, 8), f32),
            jax.ShapeDtypeStruct((1, LANES), f32)]


def _slot(seg_ref, meta_ref, k, r):
    return seg_ref[meta_ref[k, r]] + meta_ref[MOE_TOP_K + k, r]


def _dispatch_body(seg_ref, meta_ref, f_ref, xp_ref, zbuf, sem, zsem):
    i = pl.program_id(0)
    n_blocks = xp_ref.shape[0] // EXPERT_BLOCK

    @pl.when(i == 0)
    def _():
        zbuf[...] = jnp.zeros_like(zbuf)

    e = jnp.minimum(i, MOE_EXPERTS - 1)
    n_pad = jnp.where(i < MOE_EXPERTS, seg_ref[2 * MOE_EXPERTS + e], 0)
    off = seg_ref[MOE_EXPERTS + e]
    n_head = (-off) & (SUBLANES - 1)
    fills = []
    for j in range(SUBLANES - 1):
        fills.append(((j < n_head) & (n_pad > 0),
                      pltpu.make_async_copy(zbuf.at[pl.ds(0, 1), :], xp_ref.at[pl.ds(off + j, 1), :], zsem)))
    off = off + n_head
    n_tiles_pad = jnp.where(n_pad > 0, (n_pad - n_head) // SUBLANES, 0)
    for bit in PAD_PIECES:
        take = (n_tiles_pad & bit) != 0
        rows = bit * SUBLANES
        fills.append((take, pltpu.make_async_copy(
            zbuf.at[pl.ds(0, rows), :], xp_ref.at[pl.ds(pl.multiple_of(off, SUBLANES), rows), :], zsem)))
        off = off + jnp.where(take, rows, 0)
    tail_blk = seg_ref[3 * MOE_EXPERTS] + i
    has_tail = (i < MOE_EXPERTS) & (tail_blk < n_blocks)
    tail_row = jnp.minimum(tail_blk, n_blocks - 1) * EXPERT_BLOCK
    fills.append((has_tail, pltpu.make_async_copy(zbuf, xp_ref.at[pl.ds(tail_row, EXPERT_BLOCK), :], zsem)))
    for take, cp in fills:
        pl.when(take)(cp.start)

    def issue(r, carry):
        for k in range(MOE_TOP_K):
            d = _slot(seg_ref, meta_ref, k, r)
            pltpu.make_async_copy(f_ref.at[pl.ds(r, 1), :], xp_ref.at[pl.ds(d, 1), :], sem).start()
        return carry

    lax.fori_loop(0, TM, issue, 0)
    for k in range(MOE_TOP_K):
        pltpu.make_async_copy(f_ref, xp_ref.at[pl.ds(0, TM), :], sem).wait()
    for take, cp in fills:
        pl.when(take)(cp.wait)


def _dispatch_call(seg, meta, f, n_rows):
    t, d = f.shape
    grid_spec = pltpu.PrefetchScalarGridSpec(
        num_scalar_prefetch=1,
        grid=(t // TM,),
        in_specs=[pl.BlockSpec((None, META_ROWS, TM), lambda i, seg: (i, 0, 0), memory_space=pltpu.SMEM),
                  pl.BlockSpec((TM, d), lambda i, seg: (i, 0))],
        out_specs=pl.BlockSpec(memory_space=pl.ANY),
        scratch_shapes=[pltpu.VMEM((EXPERT_BLOCK, d), f32), pltpu.SemaphoreType.DMA(()),
                        pltpu.SemaphoreType.DMA(())],
    )
    return pl.pallas_call(
        _dispatch_body,
        grid_spec=grid_spec,
        out_shape=jax.ShapeDtypeStruct((n_rows, d), f32),
        compiler_params=_params("arbitrary"),
        name="moe_dispatch",
    )(seg, meta, f)


def _expert_body(be_ref, na_ref, x_ref, wg_ref, wu_ref, wd_ref, o_ref, wg_s, wu_s, wd_s):
    i = pl.program_id(0)
    active = i < na_ref[0]
    changed = (i == 0) | (be_ref[i] != be_ref[jnp.maximum(i - 1, 0)])

    @pl.when(active & changed)
    def _():
        wg_s[...] = wg_ref[...].astype(bf16)
        wu_s[...] = wu_ref[...].astype(bf16)
        wd_s[...] = wd_ref[...].astype(bf16)

    @pl.when(active)
    def _():
        x = x_ref[...].astype(bf16)
        g = jnp.dot(x, wg_s[...], preferred_element_type=f32)
        u = jnp.dot(x, wu_s[...], preferred_element_type=f32)
        h = (_silu(g) * u).astype(bf16)
        o_ref[...] = jnp.dot(h, wd_s[...], preferred_element_type=f32)

    @pl.when(jnp.logical_not(active))
    def _():
        o_ref[...] = jnp.zeros_like(o_ref)


def _expert_call(blk_e, n_act, xp, w_gate, w_up, w_down, layer):
    n_rows, d = xp.shape
    ff = w_gate.shape[-1]

    def blk(i, be, na):
        return jnp.minimum(i, na[0] - 1)

    def wmap(i, be, na):
        return (layer, be[blk(i, be, na)], 0, 0)

    grid_spec = pltpu.PrefetchScalarGridSpec(
        num_scalar_prefetch=2,
        grid=(n_rows // EXPERT_BLOCK,),
        in_specs=[pl.BlockSpec((EXPERT_BLOCK, d), lambda i, be, na: (blk(i, be, na), 0)),
                  pl.BlockSpec((None, None, d, ff), wmap),
                  pl.BlockSpec((None, None, d, ff), wmap),
                  pl.BlockSpec((None, None, ff, d), wmap)],
        out_specs=pl.BlockSpec((EXPERT_BLOCK, d), lambda i, be, na: (i, 0)),
        scratch_shapes=[pltpu.VMEM((d, ff), bf16), pltpu.VMEM((d, ff), bf16), pltpu.VMEM((ff, d), bf16)],
    )
    return pl.pallas_call(
        _expert_body,
        grid_spec=grid_spec,
        out_shape=jax.ShapeDtypeStruct((n_rows, d), f32),
        compiler_params=_params("arbitrary"),
        name="moe_experts",
    )(blk_e, n_act, xp, w_gate, w_up, w_down)


def _combine_body(seg_ref, meta_ref, x_ref, wts_ref, g2_ref, fg_ref, yp_ref, o_ref, gbuf, sem, *, final):
    def issue(r, carry):
        for k in range(MOE_TOP_K):
            d = _slot(seg_ref, meta_ref, k, r)
            pltpu.make_async_copy(yp_ref.at[pl.ds(d, 1), :], gbuf.at[k, pl.ds(r, 1), :], sem).start()
        return carry

    lax.fori_loop(0, TM, issue, 0)
    for k in range(MOE_TOP_K):
        pltpu.make_async_copy(yp_ref.at[pl.ds(0, TM), :], gbuf.at[k], sem).wait()
    w = wts_ref[...]
    y = w[:, 0:1] * gbuf[0] + w[:, 1:2] * gbuf[1]
    xn = x_ref[...] + g2_ref[...] * y
    if final:
        xn = _rms(xn) * fg_ref[...]
    o_ref[...] = xn


def _combine_call(seg, meta, x, wts, mods3, final_g, yp, n_batch, tiles_per_b, final):
    t, d = x.shape
    t0 = 1 if final else 0
    nt = tiles_per_b - t0

    def tile(b, j):
        return b * tiles_per_b + t0 + j

    def g2_map(b, j, seg):
        r = jnp.where(t0 + j == 0, n_batch, b)
        return (r * N_MOD + 5, 0, 0)

    grid_spec = pltpu.PrefetchScalarGridSpec(
        num_scalar_prefetch=1,
        grid=(n_batch, nt),
        in_specs=[pl.BlockSpec((None, META_ROWS, TM), lambda b, j, seg: (tile(b, j), 0, 0),
                               memory_space=pltpu.SMEM),
                  pl.BlockSpec((TM, d), lambda b, j, seg: (tile(b, j), 0)),
                  pl.BlockSpec((TM, 8), lambda b, j, seg: (tile(b, j), 0)),
                  pl.BlockSpec((None, 1, d), g2_map),
                  pl.BlockSpec((1, d), lambda b, j, seg: (0, 0)),
                  pl.BlockSpec(memory_space=pl.ANY)],
        out_specs=pl.BlockSpec((TM, d), lambda b, j, seg: (b * nt + j, 0)),
        scratch_shapes=[pltpu.VMEM((MOE_TOP_K, TM, d), f32), pltpu.SemaphoreType.DMA(())],
    )
    return pl.pallas_call(
        functools.partial(_combine_body, final=final),
        grid_spec=grid_spec,
        out_shape=jax.ShapeDtypeStruct((n_batch * nt * TM, d), f32),
        compiler_params=_params("arbitrary", "arbitrary"),
        name="moe_combine",
    )(seg, meta, x, wts, mods3, final_g, yp)


def _rope_tables(seq, n_ctx, half):
    inv = ROPE_THETA ** (-jnp.arange(half, dtype=f32) / half)
    t = jnp.arange(seq, dtype=jnp.int32)
    ang_r = (t // GRID_W).astype(f32)[:, None] * inv[None, :]
    ang_c = (t % GRID_W).astype(f32)[:, None] * inv[None, :]
    zero = jnp.zeros_like(ang_r)
    cos = jnp.concatenate([jnp.cos(ang_r), jnp.cos(ang_r), jnp.cos(ang_c), jnp.cos(ang_c)], axis=-1)
    s1 = jnp.concatenate([-jnp.sin(ang_r), zero, -jnp.sin(ang_c), zero], axis=-1)
    s2 = jnp.concatenate([zero, jnp.sin(ang_r), zero, jnp.sin(ang_c)], axis=-1)
    w = 4 * half
    ctx_c = jnp.ones((n_ctx, w), f32)
    ctx_s = jnp.zeros((n_ctx, w), f32)
    pad = ((0, 0), (0, LANES - w))
    return (jnp.pad(jnp.concatenate([ctx_c, cos], axis=0), pad),
            jnp.pad(jnp.concatenate([ctx_s, s1], axis=0), pad),
            jnp.pad(jnp.concatenate([ctx_s, s2], axis=0), pad))


def _table_spec(tiles_per_b):
    return pl.BlockSpec((TM, LANES), lambda i: (i % tiles_per_b, 0))


def kernel(x, c, ctx, c_ctx, w_mod, b_mod, norm_mix_g, norm_ffn_g, mla_w_dq, mla_g_q, mla_w_uq, mla_w_dkv, mla_g_kv, mla_w_ukv, mla_w_o, conv_w_in, conv_w, conv_w_out, gqa_w_qkv, gqa_q_norm_g, gqa_k_norm_g, gqa_w_o, moe_w_group, moe_b_group, moe_w_expert, moe_b_expert, moe_w_gate, moe_w_up, moe_w_down, final_norm_g):
    n_batch, seq, d = x.shape
    n_ctx = ctx.shape[1]
    depth = w_mod.shape[0]
    assert n_ctx == TM and seq % TM == 0 and d == 1024
    nt = n_ctx + seq
    tiles_per_b = nt // TM
    t_all = n_batch * nt
    n_tiles = t_all // TM
    n_assign = t_all * MOE_TOP_K
    n_blocks = -(-n_assign // EXPERT_BLOCK) + MOE_EXPERTS
    n_rows = n_blocks * EXPERT_BLOCK

    xs = jnp.concatenate([ctx, x], axis=1).reshape(t_all, d)
    cond8 = jnp.concatenate([c, c_ctx[None], jnp.zeros((8 - n_batch - 1, d), f32)], axis=0)
    mods = _mod_call(cond8, w_mod, b_mod)
    mods = mods.reshape(depth, 8 * N_MOD, 1, d)

    mla_tabs = _rope_tables(seq, n_ctx, MLA_ROPE // 4)
    gqa_tabs = _rope_tables(seq, n_ctx, GQA_HEAD_DIM // 4)
    ltri = jnp.tril(jnp.ones((TM, TM), f32), -1).astype(bf16)
    mspec = functools.partial(_mod_spec, tiles_per_b=tiles_per_b, n_batch=n_batch)
    tab_specs = [_table_spec(tiles_per_b)] * 3
    route_in, route_out = _route_specs(d, tiles_per_b, n_batch)
    route_shapes = _route_out_shapes(t_all, d)
    route_scratch = [pltpu.VMEM((1, LANES), f32)]

    for i in range(depth):
        kind, j = i % N_MIXERS, i // N_MIXERS
        m3 = mods[i]
        gmix = norm_mix_g[i][None]
        head_in = [_tile_spec(d), mspec(0), mspec(1), _row_spec(d)]
        wr = jnp.pad(jnp.concatenate([moe_w_group[i], moe_w_expert[i]], axis=1),
                     ((0, 0), (0, LANES - MOE_GROUPS - MOE_EXPERTS)))
        br = jnp.pad(jnp.concatenate([moe_b_group[i], moe_b_expert[i]]),
                     (0, LANES - MOE_GROUPS - MOE_EXPERTS))[None]
        route_args = (m3, norm_ffn_g[i][None], m3, m3, wr, br, ltri)

        if kind == 0:
            dk, dv = 2 * LANES, MLA_V
            wcat = jnp.pad(jnp.concatenate([mla_w_dq[j], mla_w_dkv[j]], axis=1),
                           ((0, 0), (0, LANES - MLA_ROPE))).astype(bf16)
            wuq = jnp.pad(mla_w_uq[j].reshape(MLA_Q_LORA, MLA_HEADS, MLA_NOPE + MLA_ROPE),
                          ((0, 0), (0, 0), (0, dk - MLA_NOPE - MLA_ROPE))
                          ).reshape(MLA_Q_LORA, MLA_HEADS * dk).astype(bf16)
            wukv3 = mla_w_ukv[j].reshape(MLA_KV_LORA, MLA_HEADS, MLA_NOPE + MLA_V)
            wukv = jnp.concatenate([wukv3[..., :MLA_NOPE].reshape(MLA_KV_LORA, -1),
                                    wukv3[..., MLA_NOPE:].reshape(MLA_KV_LORA, -1)], axis=1).astype(bf16)
            q, k, v = pl.pallas_call(
                functools.partial(_mla_proj_body, scale=(MLA_NOPE + MLA_ROPE) ** -0.5 * LOG2E),
                grid=(n_tiles,),
                in_specs=head_in + [_full_spec(wcat.shape), _row_spec(MLA_Q_LORA), _row_spec(MLA_KV_LORA),
                                    _full_spec(wuq.shape), _full_spec(wukv.shape)] + tab_specs,
                out_specs=[_tile_spec(MLA_HEADS * dk), _tile_spec(MLA_HEADS * dk), _tile_spec(MLA_HEADS * dv)],
                out_shape=[jax.ShapeDtypeStruct((t_all, MLA_HEADS * dk), bf16),
                           jax.ShapeDtypeStruct((t_all, MLA_HEADS * dk), bf16),
                           jax.ShapeDtypeStruct((t_all, MLA_HEADS * dv), bf16)],
                compiler_params=_params("arbitrary"),
                name="mla_proj",
            )(xs, m3, m3, gmix, wcat, mla_g_q[j][None], mla_g_kv[j][None], wuq, wukv, *mla_tabs)
            att = _attn_call(q, k, v, n_batch, MLA_HEADS, MLA_HEADS, dk, dv, n_ctx)
            w_o = mla_w_o[j].astype(bf16)
        elif kind == 2:
            dk = dv = GQA_HEAD_DIM
            wqkv = gqa_w_qkv[j].astype(bf16)
            q, k, v = pl.pallas_call(
                functools.partial(_gqa_proj_body, scale=GQA_HEAD_DIM ** -0.5 * LOG2E),
                grid=(n_tiles,),
                in_specs=head_in + [_full_spec(wqkv.shape), _row_spec(dk), _row_spec(dk)] + tab_specs,
                out_specs=[_tile_spec(GQA_HEADS * dk), _tile_spec(GQA_KV_HEADS * dk),
                           _tile_spec(GQA_KV_HEADS * dv)],
                out_shape=[jax.ShapeDtypeStruct((t_all, GQA_HEADS * dk), bf16),
                           jax.ShapeDtypeStruct((t_all, GQA_KV_HEADS * dk), bf16),
                           jax.ShapeDtypeStruct((t_all, GQA_KV_HEADS * dv), bf16)],
                compiler_params=_params("arbitrary"),
                name="gqa_proj",
            )(xs, m3, m3, gmix, wqkv, gqa_q_norm_g[j][None], gqa_k_norm_g[j][None], *gqa_tabs)
            att = _attn_call(q, k, v, n_batch, GQA_HEADS, GQA_KV_HEADS, dk, dv, n_ctx)
            w_o = gqa_w_o[j].astype(bf16)

        if kind == 1:
            w_in = conv_w_in[j].astype(bf16)
            bg, cu = pl.pallas_call(
                _conv_proj_body,
                grid=(n_tiles,),
                in_specs=head_in + [_full_spec(w_in.shape)],
                out_specs=[_tile_spec(d), _tile_spec(d)],
                out_shape=[jax.ShapeDtypeStruct((t_all, d), bf16)] * 2,
                compiler_params=_params("arbitrary"),
                name="conv_proj",
            )(xs, m3, m3, gmix, w_in)
            w_o = conv_w_out[j].astype(bf16)
            per = TM // HALO
            last = t_all // HALO - 1
            xs, f, meta, wts, cnt = pl.pallas_call(
                functools.partial(_conv_out_body, tiles_per_b=tiles_per_b),
                grid=(n_tiles,),
                in_specs=[_tile_spec(d),
                          pl.BlockSpec((HALO, d), lambda i: (jnp.maximum(i * per - 1, 0), 0)),
                          pl.BlockSpec((HALO, d), lambda i: (jnp.minimum((i + 1) * per, last), 0)),
                          _tile_spec(d), _full_spec((3, d)), _full_spec(w_o.shape), _tile_spec(d)] + route_in,
                out_specs=route_out,
                out_shape=route_shapes,
                scratch_shapes=route_scratch,
                compiler_params=_params("arbitrary"),
                name="conv_out_route",
            )(cu, cu, cu, bg, conv_w[j], w_o, xs, *route_args)
        else:
            xs, f, meta, wts, cnt = pl.pallas_call(
                _attn_out_body,
                grid=(n_tiles,),
                in_specs=[_tile_spec(att.shape[-1]), _full_spec(w_o.shape), _tile_spec(d)] + route_in,
                out_specs=route_out,
                out_shape=route_shapes,
                scratch_shapes=route_scratch,
                compiler_params=_params("arbitrary"),
                name="attn_out_route",
            )(att, w_o, xs, *route_args)

        counts = cnt[0, ROUTE_LANE0:ROUTE_LANE0 + MOE_EXPERTS].astype(jnp.int32)
        padded = (counts + EXPERT_BLOCK - 1) // EXPERT_BLOCK * EXPERT_BLOCK
        pends = jnp.cumsum(padded)
        pstarts = pends - padded
        blk_row0 = jnp.arange(n_blocks, dtype=jnp.int32) * EXPERT_BLOCK
        blk_e = jnp.minimum(jnp.sum((pends[None, :] <= blk_row0[:, None]).astype(jnp.int32), axis=1),
                            MOE_EXPERTS - 1)
        n_act = (pends[-1:] // EXPERT_BLOCK).astype(jnp.int32)
        seg = jnp.concatenate([pstarts, pstarts + counts, padded - counts, n_act,
                               jnp.zeros((MOE_EXPERTS - 1,), jnp.int32)]).astype(jnp.int32)

        xp = _dispatch_call(seg, meta, f, n_rows)
        yp = _expert_call(blk_e, n_act, xp, moe_w_gate, moe_w_up, moe_w_down, i)
        final = i == depth - 1
        xs = _combine_call(seg, meta, xs, wts, m3, final_norm_g[None], yp, n_batch, tiles_per_b, final)

    return xs.reshape(n_batch, seq, d)
```

```python
import functools

import jax
import jax.numpy as jnp
from jax import lax
from jax.experimental import pallas as pl
from jax.experimental.pallas import tpu as pltpu

f32 = jnp.float32
bf16 = jnp.bfloat16

GRID_W = 64
ROPE_THETA = 10000.0
NORM_EPS = 1e-6
N_MOD = 6
N_MIXERS = 3
MLA_HEADS, MLA_Q_LORA, MLA_KV_LORA, MLA_NOPE, MLA_ROPE, MLA_V = 8, 512, 256, 128, 64, 128
GQA_HEADS, GQA_KV_HEADS, GQA_HEAD_DIM = 16, 8, 128
MOE_GROUPS, MOE_EXPERTS_PER_GROUP, MOE_TOP_K = 4, 8, 2
MOE_EXPERTS = MOE_GROUPS * MOE_EXPERTS_PER_GROUP

LANES = 128
TM = 256
EXPERT_BLOCK = 256
SUBLANES = 8
RUN_ALIGN = 8
PAD_PIECES = tuple(1 << b for b in reversed(range(EXPERT_BLOCK.bit_length())))
HALO = 16
ATTN_KEY_CHUNK = 256
ATTN_HEADS_PER_STEP = 4
LOG2E = 1.4426950408889634
VMEM_LIMIT = 56 * 1024 * 1024
ROUTE_LANE0 = MOE_GROUPS
META_ROWS = 8
MXU_TILE = 256
LOCAL_ROWS = -(-(TM * MOE_TOP_K + MOE_EXPERTS * (RUN_ALIGN - 1)) // MXU_TILE) * MXU_TILE
NEG = -1e30


def _params(*sem):
    return pltpu.CompilerParams(dimension_semantics=sem, vmem_limit_bytes=VMEM_LIMIT)


def _rms(x):
    return x * lax.rsqrt(jnp.mean(x * x, axis=-1, keepdims=True) + NORM_EPS)


def _modulate(x, gn, sh, sc):
    return _rms(x) * gn * (1.0 + sc) + sh


def _silu(x):
    return x / (1.0 + jnp.exp(-x))


def _rope(x, c, s1, s2, shift):
    return x * c + pltpu.roll(x, LANES - shift, 1) * s1 + pltpu.roll(x, shift, 1) * s2


def _mod_body(cond_ref, w_ref, b_ref, o_ref):
    a = _silu(cond_ref[...])
    o_ref[...] = jnp.dot(a, w_ref[...], preferred_element_type=f32,
                         precision=lax.Precision.HIGHEST) + b_ref[...]


def _mod_call(cond8, w_mod, b_mod):
    depth, d, n6 = w_mod.shape
    tn = n6 // 4
    return pl.pallas_call(
        _mod_body,
        grid=(depth, n6 // tn),
        in_specs=[pl.BlockSpec((8, d), lambda l, j: (0, 0)),
                  pl.BlockSpec((None, d, tn), lambda l, j: (l, 0, j)),
                  pl.BlockSpec((None, 1, tn), lambda l, j: (l, 0, j))],
        out_specs=pl.BlockSpec((None, 8, tn), lambda l, j: (l, 0, j)),
        out_shape=jax.ShapeDtypeStruct((depth, 8, n6), f32),
        compiler_params=_params("arbitrary", "arbitrary"),
        name="mod_vectors",
    )(cond8, w_mod, b_mod.reshape(depth, 1, n6))


def _mod_spec(chunk, tiles_per_b, n_batch):
    def imap(i):
        r = jnp.where(i % tiles_per_b == 0, n_batch, i // tiles_per_b)
        return (r * N_MOD + chunk, 0, 0)
    return pl.BlockSpec((None, 1, 1024), imap)


def _row_spec(n):
    return pl.BlockSpec((1, n), lambda i: (0, 0))


def _full_spec(shape):
    return pl.BlockSpec(shape, lambda i: (0,) * len(shape))


def _tile_spec(n):
    return pl.BlockSpec((TM, n), lambda i: (i, 0))


def _mla_proj_body(x_ref, sh_ref, sc_ref, gn_ref, wcat_ref, gq_ref, gkv_ref, wuq_ref, wukv_ref,
                   c_ref, s1_ref, s2_ref, q_ref, k_ref, v_ref, *, scale):
    a = _modulate(x_ref[...], gn_ref[...], sh_ref[...], sc_ref[...]).astype(bf16)
    z = jnp.dot(a, wcat_ref[...], preferred_element_type=f32)
    cq = (_rms(z[:, :MLA_Q_LORA]) * gq_ref[...]).astype(bf16)
    ckv = (_rms(z[:, MLA_Q_LORA:MLA_Q_LORA + MLA_KV_LORA]) * gkv_ref[...]).astype(bf16)
    kpe = z[:, MLA_Q_LORA + MLA_KV_LORA:]
    q = jnp.dot(cq, wuq_ref[...], preferred_element_type=f32)
    kv = jnp.dot(ckv, wukv_ref[...], preferred_element_type=f32)
    c, s1, s2 = c_ref[...], s1_ref[...], s2_ref[...]
    half = MLA_ROPE // 4
    kpe_r = _rope(kpe, c, s1, s2, half).astype(bf16)
    hd = 2 * LANES
    for h in range(MLA_HEADS):
        q_ref[:, h * hd:h * hd + LANES] = (q[:, h * hd:h * hd + LANES] * scale).astype(bf16)
        q_ref[:, h * hd + LANES:(h + 1) * hd] = (
            _rope(q[:, h * hd + LANES:(h + 1) * hd], c, s1, s2, half) * scale).astype(bf16)
        k_ref[:, h * hd:h * hd + LANES] = kv[:, h * LANES:(h + 1) * LANES].astype(bf16)
        k_ref[:, h * hd + LANES:(h + 1) * hd] = kpe_r
    v_ref[...] = kv[:, MLA_HEADS * MLA_NOPE:].astype(bf16)


def _gqa_proj_body(x_ref, sh_ref, sc_ref, gn_ref, w_ref, qg_ref, kg_ref,
                   c_ref, s1_ref, s2_ref, q_ref, k_ref, v_ref, *, scale):
    a = _modulate(x_ref[...], gn_ref[...], sh_ref[...], sc_ref[...]).astype(bf16)
    z = jnp.dot(a, w_ref[...], preferred_element_type=f32)
    c, s1, s2 = c_ref[...], s1_ref[...], s2_ref[...]
    half = GQA_HEAD_DIM // 4
    dq = GQA_HEADS * GQA_HEAD_DIM
    dkv = GQA_KV_HEADS * GQA_HEAD_DIM
    qg = qg_ref[...] * scale
    kg = kg_ref[...]
    for h in range(GQA_HEADS):
        qh = _rms(z[:, h * LANES:(h + 1) * LANES]) * qg
        q_ref[:, h * LANES:(h + 1) * LANES] = _rope(qh, c, s1, s2, half).astype(bf16)
    for h in range(GQA_KV_HEADS):
        kh = _rms(z[:, dq + h * LANES:dq + (h + 1) * LANES]) * kg
        k_ref[:, h * LANES:(h + 1) * LANES] = _rope(kh, c, s1, s2, half).astype(bf16)
    v_ref[...] = z[:, dq + dkv:].astype(bf16)


def _conv_proj_body(x_ref, sh_ref, sc_ref, gn_ref, w_ref, bg_ref, cu_ref):
    a = _modulate(x_ref[...], gn_ref[...], sh_ref[...], sc_ref[...]).astype(bf16)
    z = jnp.dot(a, w_ref[...], preferred_element_type=f32)
    d = bg_ref.shape[-1]
    bg_ref[...] = z[:, :d].astype(bf16)
    cu_ref[...] = (z[:, d:2 * d] * z[:, 2 * d:]).astype(bf16)


def _lane_groups(x, op):
    parts = [x[:, i * LANES:(i + 1) * LANES] for i in range(x.shape[1] // LANES)]
    while len(parts) > 1:
        parts = [op(parts[i], parts[i + 1]) if i + 1 < len(parts) else parts[i]
                 for i in range(0, len(parts), 2)]
    return parts[0]


def _attn_body(q_ref, k_ref, v_ref, o_ref, s_scr, *, n_ctx, n_all, heads_per_step, group, dk, dv):
    def run(nk):
        chunks = [(a, min(a + ATTN_KEY_CHUNK, nk)) for a in range(0, nk, ATTN_KEY_CHUNK)]
        for j in range(heads_per_step):
            kv = j // group
            q = q_ref[:, j * dk:(j + 1) * dk]
            m = None
            for a, b in chunks:
                s = lax.dot_general(q, k_ref[a:b, kv * dk:(kv + 1) * dk], (((1,), (1,)), ((), ())),
                                    preferred_element_type=f32)
                s_scr[j, :, a:b] = s
                mc = _lane_groups(s, jnp.maximum)
                m = mc if m is None else jnp.maximum(m, mc)
            m = jnp.max(m, axis=-1, keepdims=True)
            o = l = None
            for a, b in chunks:
                p = jnp.exp2(s_scr[j, :, a:b] - m)
                lc = _lane_groups(p, jnp.add)
                oc = jnp.dot(p.astype(bf16), v_ref[a:b, kv * dv:(kv + 1) * dv], preferred_element_type=f32)
                o, l = (oc, lc) if o is None else (o + oc, l + lc)
            l = jnp.sum(l, axis=-1, keepdims=True)
            o_ref[:, j * dv:(j + 1) * dv] = (o / l).astype(o_ref.dtype)

    i = pl.program_id(2)
    pl.when(i == 0)(lambda: run(n_ctx))
    pl.when(i > 0)(lambda: run(n_all))


def _attn_call(q, k, v, n_batch, heads, kv_heads, dk, dv, n_ctx):
    nt = q.shape[0] // n_batch
    g = heads // kv_heads
    hp = ATTN_HEADS_PER_STEP
    kh = hp // g
    q3, k3, v3 = (a.reshape(n_batch, nt, a.shape[-1]) for a in (q, k, v))
    out = pl.pallas_call(
        functools.partial(_attn_body, n_ctx=n_ctx, n_all=nt, heads_per_step=hp, group=g, dk=dk, dv=dv),
        grid=(n_batch, heads // hp, nt // TM),
        in_specs=[pl.BlockSpec((None, TM, hp * dk), lambda b, h, i: (b, i, h)),
                  pl.BlockSpec((None, nt, kh * dk), lambda b, h, i: (b, 0, h)),
                  pl.BlockSpec((None, nt, kh * dv), lambda b, h, i: (b, 0, h))],
        out_specs=pl.BlockSpec((None, TM, hp * dv), lambda b, h, i: (b, i, h)),
        out_shape=jax.ShapeDtypeStruct((n_batch, nt, heads * dv), bf16),
        scratch_shapes=[pltpu.VMEM((hp, TM, nt), f32)],
        compiler_params=_params("arbitrary", "arbitrary", "arbitrary"),
        name="attention",
    )(q3, k3, v3)
    return out.reshape(n_batch * nt, heads * dv)


def _route_epilogue(x, y, g1_ref, gn2_ref, sh2_ref, sc2_ref, wr_ref, br_ref, ltri_ref, upper_ref,
                    xo_ref, f_ref, meta_ref, wts_ref, runs_ref, cnt_ref, carry_ref):
    i = pl.program_id(0)

    @pl.when(i == 0)
    def _():
        carry_ref[...] = jnp.zeros_like(carry_ref)

    xn = x + g1_ref[...] * y
    xo_ref[...] = xn
    f = _modulate(xn, gn2_ref[...], sh2_ref[...], sc2_ref[...])
    f_ref[...] = f
    logits = jnp.dot(f, wr_ref[...], preferred_element_type=f32, precision=lax.Precision.HIGHEST)
    lane = lax.broadcasted_iota(jnp.int32, logits.shape, 1).astype(f32)
    bias = br_ref[...]

    def first_argmax(score):
        top = jnp.max(score, axis=-1, keepdims=True)
        return jnp.min(jnp.where(score == top, lane, float(LANES)), axis=-1, keepdims=True)

    def masked_softmax(mask):
        z = jnp.where(mask, logits, NEG)
        e = jnp.exp(z - jnp.max(z, axis=-1, keepdims=True))
        return e / jnp.sum(e, axis=-1, keepdims=True)

    def pick(onehot, v):
        return jnp.sum(jnp.where(onehot, v, 0.0), axis=-1, keepdims=True)

    gmask = lane < float(MOE_GROUPS)
    gp = masked_softmax(gmask)
    g_sel = first_argmax(jnp.where(gmask, gp + bias, NEG))
    g_w = pick(lane == g_sel, gp)
    lo = float(ROUTE_LANE0) + float(MOE_EXPERTS_PER_GROUP) * g_sel
    emask = (lane >= lo) & (lane < lo + float(MOE_EXPERTS_PER_GROUP))
    ep = masked_softmax(emask)
    escore = jnp.where(emask, ep + bias, NEG)
    i1 = first_argmax(escore)
    oh1 = lane == i1
    i2 = first_argmax(jnp.where(oh1, NEG, escore))
    oh2 = lane == i2
    p1, p2 = pick(oh1, ep), pick(oh2, ep)
    w1 = p1 / (p1 + p2) * g_w
    w2 = p2 / (p1 + p2) * g_w

    hits = jnp.where(oh1 | oh2, 1.0, 0.0)
    before = jnp.dot(ltri_ref[...], hits.astype(bf16), preferred_element_type=f32)
    n_tile = jnp.sum(hits, axis=0, keepdims=True)
    n_al = jnp.floor((n_tile + float(RUN_ALIGN - 1)) * (1.0 / RUN_ALIGN)) * float(RUN_ALIGN)
    lo = jnp.dot(jnp.broadcast_to(n_al, (SUBLANES, LANES)), upper_ref[...], preferred_element_type=f32,
                 precision=lax.Precision.HIGHEST)[0:1, :]
    lp1, lp2 = pick(oh1, before + lo), pick(oh2, before + lo)

    row = lax.broadcasted_iota(jnp.int32, runs_ref.shape, 0)
    runs = jnp.where(row == 0, n_tile, jnp.where(row == 1, carry_ref[...], jnp.where(row == 2, lo, 0.0)))
    runs_ref[...] = runs.astype(jnp.int32)
    carry_ref[...] = carry_ref[...] + n_tile
    cnt_ref[...] = carry_ref[...]

    meta = jnp.where(lane == 0.0, lp1, jnp.where(lane == 1.0, lp2, 0.0))
    meta_ref[...] = meta.T[:meta_ref.shape[0], :].astype(jnp.int32)
    col = lax.broadcasted_iota(jnp.int32, wts_ref.shape, 1)
    wts_ref[...] = jnp.where(col == 0, w1, jnp.where(col == 1, w2, jnp.where(col == 2, lp1, lp2)))


def _attn_out_body(y_ref, wo_ref, x_ref, *rest):
    y = jnp.dot(y_ref[...], wo_ref[...], preferred_element_type=f32)
    _route_epilogue(x_ref[...], y, *rest)


def _conv_out_body(cu_ref, prev_ref, next_ref, bg_ref, cw_ref, wo_ref, x_ref, *rest, tiles_per_b):
    t = pl.program_id(0) % tiles_per_b
    has_prev = (t >= 2).astype(f32)
    has_next = ((t != 0) & (t != tiles_per_b - 1)).astype(f32)
    cu = cu_ref[...].astype(f32)
    row = lax.broadcasted_iota(jnp.int32, cu.shape, 0)
    hp = prev_ref[HALO - 1:HALO, :].astype(f32) * has_prev
    hn = next_ref[0:1, :].astype(f32) * has_next
    up = jnp.where(row == 0, hp, pltpu.roll(cu, 1, 0))
    dn = jnp.where(row == TM - 1, hn, pltpu.roll(cu, TM - 1, 0))
    cw = cw_ref[...]
    z = cw[0:1, :] * up + cw[1:2, :] * cu + cw[2:3, :] * dn
    yb = (bg_ref[...].astype(f32) * z).astype(bf16)
    y = jnp.dot(yb, wo_ref[...], preferred_element_type=f32)
    _route_epilogue(x_ref[...], y, *rest)


def _route_specs(d, tiles_per_b, n_batch):
    ins = [_mod_spec(2, tiles_per_b, n_batch), _row_spec(d), _mod_spec(3, tiles_per_b, n_batch),
           _mod_spec(4, tiles_per_b, n_batch), _full_spec((d, LANES)), _row_spec(LANES),
           _full_spec((TM, TM)), _full_spec((LANES, LANES))]
    outs = [_tile_spec(d), _tile_spec(d), pl.BlockSpec((None, META_ROWS, TM), lambda i: (i, 0, 0)),
            _tile_spec(8), pl.BlockSpec((None, SUBLANES, LANES), lambda i: (i, 0, 0)), _row_spec(LANES)]
    return ins, outs


def _route_out_shapes(t, d):
    return [jax.ShapeDtypeStruct((t, d), f32), jax.ShapeDtypeStruct((t, d), f32),
            jax.ShapeDtypeStruct((t // TM, META_ROWS, TM), jnp.int32), jax.ShapeDtypeStruct((t, 8), f32),
            jax.ShapeDtypeStruct((t // TM, SUBLANES, LANES), jnp.int32),
            jax.ShapeDtypeStruct((1, LANES), f32)]


def _rows_to_tiles(tref, x):
    n = x.shape[0]
    for s in range(SUBLANES):
        tref[pl.ds(s, n, stride=SUBLANES), :] = x[:, s * LANES:(s + 1) * LANES]


def _tiles_to_rows(tref):
    n = tref.shape[0] // SUBLANES
    return jnp.concatenate([tref[pl.ds(s, n, stride=SUBLANES), :] for s in range(SUBLANES)], axis=-1)


def _tile_rows(ref, row, n):
    return ref.at[pl.ds(row * SUBLANES, n * SUBLANES)]


def _run_copies(seg_ref, runs_ref, copy):
    def per_expert(e, total):
        lane = ROUTE_LANE0 + e
        n_chunks = (runs_ref[0, lane] + RUN_ALIGN - 1) // RUN_ALIGN
        slot0 = seg_ref[e] + runs_ref[1, lane]
        local0 = runs_ref[2, lane]

        def chunk(c, carry):
            copy(local0 + c * RUN_ALIGN, slot0 + c * RUN_ALIGN).start()
            return carry

        lax.fori_loop(0, n_chunks, chunk, 0)
        return total + n_chunks

    return lax.fori_loop(0, MOE_EXPERTS, per_expert, 0)


def _wait_chunks(n_chunks, copy):
    def wait_one(c, carry):
        copy(0, 0).wait()
        return carry
    lax.fori_loop(0, n_chunks, wait_one, 0)


def _dispatch_body(seg_ref, runs_ref, meta_ref, f_ref, xp_ref, sbuf, zbuf, sem, zsem):
    i = pl.program_id(0)
    n_blocks = xp_ref.shape[0] // (EXPERT_BLOCK * SUBLANES)

    @pl.when(i == 0)
    def _():
        zbuf[...] = jnp.zeros_like(zbuf)

        def pad_fill(e, wait):
            n_pad = seg_ref[2 * MOE_EXPERTS + e]
            off = seg_ref[MOE_EXPERTS + e]
            for piece in PAD_PIECES:
                take = (n_pad & piece) != 0
                cp = pltpu.make_async_copy(_tile_rows(zbuf, 0, piece), _tile_rows(xp_ref, off, piece), zsem)
                pl.when(take)(cp.wait if wait else cp.start)
                off = off + jnp.where(take, piece, 0)

        def tail_fill(b, wait):
            cp = pltpu.make_async_copy(zbuf, _tile_rows(xp_ref, b * EXPERT_BLOCK, EXPERT_BLOCK), zsem)
            cp.wait() if wait else cp.start()

        for wait in (False, True):
            lax.fori_loop(0, MOE_EXPERTS, lambda e, c, w=wait: (pad_fill(e, w), c)[1], 0)
            lax.fori_loop(seg_ref[3 * MOE_EXPERTS], n_blocks, lambda b, c, w=wait: (tail_fill(b, w), c)[1], 0)

    pos = lax.broadcasted_iota(jnp.int32, (LOCAL_ROWS, TM), 0)
    m = meta_ref[...]
    perm = jnp.where((pos == m[0:1, :]) | (pos == m[1:2, :]), 1.0, 0.0).astype(bf16)
    _rows_to_tiles(sbuf, jnp.dot(perm, f_ref[...].astype(bf16), preferred_element_type=f32))

    def copy(local_row, slot_row):
        return pltpu.make_async_copy(_tile_rows(sbuf, local_row, RUN_ALIGN),
                                     _tile_rows(xp_ref, slot_row, RUN_ALIGN), sem)

    _wait_chunks(_run_copies(seg_ref, runs_ref, copy), copy)


def _dispatch_call(seg, runs, meta, f, n_rows):
    t, d = f.shape
    grid_spec = pltpu.PrefetchScalarGridSpec(
        num_scalar_prefetch=1,
        grid=(t // TM,),
        in_specs=[pl.BlockSpec((None, SUBLANES, LANES), lambda i, seg: (i, 0, 0), memory_space=pltpu.SMEM),
                  pl.BlockSpec((None, META_ROWS, TM), lambda i, seg: (i, 0, 0)),
                  pl.BlockSpec((TM, d), lambda i, seg: (i, 0))],
        out_specs=pl.BlockSpec(memory_space=pl.ANY),
        scratch_shapes=[pltpu.VMEM((LOCAL_ROWS * SUBLANES, LANES), f32),
                        pltpu.VMEM((EXPERT_BLOCK * SUBLANES, LANES), f32),
                        pltpu.SemaphoreType.DMA(()), pltpu.SemaphoreType.DMA(())],
    )
    return pl.pallas_call(
        _dispatch_body,
        grid_spec=grid_spec,
        out_shape=jax.ShapeDtypeStruct((n_rows * SUBLANES, LANES), f32),
        compiler_params=_params("arbitrary"),
        name="moe_dispatch",
    )(seg, runs, meta, f)


def _expert_body(be_ref, na_ref, x_ref, wg_ref, wu_ref, wd_ref, o_ref, wg_s, wu_s, wd_s):
    i = pl.program_id(0)
    active = i < na_ref[0]
    changed = (i == 0) | (be_ref[i] != be_ref[jnp.maximum(i - 1, 0)])

    @pl.when(active & changed)
    def _():
        wg_s[...] = wg_ref[...].astype(bf16)
        wu_s[...] = wu_ref[...].astype(bf16)
        wd_s[...] = wd_ref[...].astype(bf16)

    @pl.when(active)
    def _():
        x = _tiles_to_rows(x_ref).astype(bf16)
        g = jnp.dot(x, wg_s[...], preferred_element_type=f32)
        u = jnp.dot(x, wu_s[...], preferred_element_type=f32)
        h = (_silu(g) * u).astype(bf16)
        _rows_to_tiles(o_ref, jnp.dot(h, wd_s[...], preferred_element_type=f32))

    @pl.when(jnp.logical_not(active))
    def _():
        o_ref[...] = jnp.zeros_like(o_ref)


def _expert_call(blk_e, n_act, xp, w_gate, w_up, w_down, layer):
    n_rows = xp.shape[0] // SUBLANES
    d, ff = w_gate.shape[-2:]
    blk_rows = EXPERT_BLOCK * SUBLANES

    def blk(i, be, na):
        return jnp.maximum(jnp.minimum(i, na[0] - 1), 0)

    def wmap(i, be, na):
        return (layer, be[blk(i, be, na)], 0, 0)

    grid_spec = pltpu.PrefetchScalarGridSpec(
        num_scalar_prefetch=2,
        grid=(n_rows // EXPERT_BLOCK,),
        in_specs=[pl.BlockSpec((blk_rows, LANES), lambda i, be, na: (blk(i, be, na), 0)),
                  pl.BlockSpec((None, None, d, ff), wmap),
                  pl.BlockSpec((None, None, d, ff), wmap),
                  pl.BlockSpec((None, None, ff, d), wmap)],
        out_specs=pl.BlockSpec((blk_rows, LANES), lambda i, be, na: (i, 0)),
        scratch_shapes=[pltpu.VMEM((d, ff), bf16), pltpu.VMEM((d, ff), bf16), pltpu.VMEM((ff, d), bf16)],
    )
    return pl.pallas_call(
        _expert_body,
        grid_spec=grid_spec,
        out_shape=jax.ShapeDtypeStruct((n_rows * SUBLANES, LANES), f32),
        compiler_params=_params("arbitrary"),
        name="moe_experts",
    )(blk_e, n_act, xp, w_gate, w_up, w_down)


def _combine_body(seg_ref, runs_ref, x_ref, wts_ref, g2_ref, fg_ref, yp_ref, o_ref, gbuf, sem, *, final):
    @pl.when((pl.program_id(0) == 0) & (pl.program_id(1) == 0))
    def _():
        gbuf[...] = jnp.zeros_like(gbuf)

    def copy(local_row, slot_row):
        return pltpu.make_async_copy(_tile_rows(yp_ref, slot_row, RUN_ALIGN),
                                     _tile_rows(gbuf, local_row, RUN_ALIGN), sem)

    _wait_chunks(_run_copies(seg_ref, runs_ref, copy), copy)
    ys = _tiles_to_rows(gbuf).astype(bf16)
    w = wts_ref[...]
    pos = lax.broadcasted_iota(jnp.int32, (TM, LOCAL_ROWS), 1).astype(f32)
    y = None
    for k in range(MOE_TOP_K):
        pick = jnp.where(pos == w[:, MOE_TOP_K + k:MOE_TOP_K + k + 1], 1.0, 0.0).astype(bf16)
        yk = w[:, k:k + 1] * jnp.dot(pick, ys, preferred_element_type=f32)
        y = yk if y is None else y + yk
    xn = x_ref[...] + g2_ref[...] * y
    if final:
        xn = _rms(xn) * fg_ref[...]
    o_ref[...] = xn


def _combine_call(seg, runs, x, wts, mods3, final_g, yp, n_batch, tiles_per_b, final):
    t, d = x.shape
    t0 = 1 if final else 0
    nt = tiles_per_b - t0

    def tile(b, j):
        return b * tiles_per_b + t0 + j

    def g2_map(b, j, seg):
        r = jnp.where(t0 + j == 0, n_batch, b)
        return (r * N_MOD + 5, 0, 0)

    grid_spec = pltpu.PrefetchScalarGridSpec(
        num_scalar_prefetch=1,
        grid=(n_batch, nt),
        in_specs=[pl.BlockSpec((None, SUBLANES, LANES), lambda b, j, seg: (tile(b, j), 0, 0),
                               memory_space=pltpu.SMEM),
                  pl.BlockSpec((TM, d), lambda b, j, seg: (tile(b, j), 0)),
                  pl.BlockSpec((TM, 8), lambda b, j, seg: (tile(b, j), 0)),
                  pl.BlockSpec((None, 1, d), g2_map),
                  pl.BlockSpec((1, d), lambda b, j, seg: (0, 0)),
                  pl.BlockSpec(memory_space=pl.ANY)],
        out_specs=pl.BlockSpec((TM, d), lambda b, j, seg: (b * nt + j, 0)),
        scratch_shapes=[pltpu.VMEM((LOCAL_ROWS * SUBLANES, LANES), f32), pltpu.SemaphoreType.DMA(())],
    )
    return pl.pallas_call(
        functools.partial(_combine_body, final=final),
        grid_spec=grid_spec,
        out_shape=jax.ShapeDtypeStruct((n_batch * nt * TM, d), f32),
        compiler_params=_params("arbitrary", "arbitrary"),
        name="moe_combine",
    )(seg, runs, x, wts, mods3, final_g, yp)


def _rope_tables(seq, n_ctx, half):
    inv = ROPE_THETA ** (-jnp.arange(half, dtype=f32) / half)
    t = jnp.arange(seq, dtype=jnp.int32)
    ang_r = (t // GRID_W).astype(f32)[:, None] * inv[None, :]
    ang_c = (t % GRID_W).astype(f32)[:, None] * inv[None, :]
    zero = jnp.zeros_like(ang_r)
    cos = jnp.concatenate([jnp.cos(ang_r), jnp.cos(ang_r), jnp.cos(ang_c), jnp.cos(ang_c)], axis=-1)
    s1 = jnp.concatenate([-jnp.sin(ang_r), zero, -jnp.sin(ang_c), zero], axis=-1)
    s2 = jnp.concatenate([zero, jnp.sin(ang_r), zero, jnp.sin(ang_c)], axis=-1)
    w = 4 * half
    ctx_c = jnp.ones((n_ctx, w), f32)
    ctx_s = jnp.zeros((n_ctx, w), f32)
    pad = ((0, 0), (0, LANES - w))
    return (jnp.pad(jnp.concatenate([ctx_c, cos], axis=0), pad),
            jnp.pad(jnp.concatenate([ctx_s, s1], axis=0), pad),
            jnp.pad(jnp.concatenate([ctx_s, s2], axis=0), pad))


def _table_spec(tiles_per_b):
    return pl.BlockSpec((TM, LANES), lambda i: (i % tiles_per_b, 0))


def kernel(x, c, ctx, c_ctx, w_mod, b_mod, norm_mix_g, norm_ffn_g, mla_w_dq, mla_g_q, mla_w_uq, mla_w_dkv, mla_g_kv, mla_w_ukv, mla_w_o, conv_w_in, conv_w, conv_w_out, gqa_w_qkv, gqa_q_norm_g, gqa_k_norm_g, gqa_w_o, moe_w_group, moe_b_group, moe_w_expert, moe_b_expert, moe_w_gate, moe_w_up, moe_w_down, final_norm_g):
    n_batch, seq, d = x.shape
    n_ctx = ctx.shape[1]
    depth = w_mod.shape[0]
    assert n_ctx == TM and seq % TM == 0 and d == 1024
    nt = n_ctx + seq
    tiles_per_b = nt // TM
    t_all = n_batch * nt
    n_tiles = t_all // TM
    n_assign = t_all * MOE_TOP_K
    n_blocks = -(-(n_assign + MOE_EXPERTS * (RUN_ALIGN - 1)) // EXPERT_BLOCK) + MOE_EXPERTS
    n_rows = n_blocks * EXPERT_BLOCK

    xs = jnp.concatenate([ctx, x], axis=1).reshape(t_all, d)
    cond8 = jnp.concatenate([c, c_ctx[None], jnp.zeros((8 - n_batch - 1, d), f32)], axis=0)
    mods = _mod_call(cond8, w_mod, b_mod)
    mods = mods.reshape(depth, 8 * N_MOD, 1, d)

    mla_tabs = _rope_tables(seq, n_ctx, MLA_ROPE // 4)
    gqa_tabs = _rope_tables(seq, n_ctx, GQA_HEAD_DIM // 4)
    ltri = jnp.tril(jnp.ones((TM, TM), f32), -1).astype(bf16)
    upper = jnp.triu(jnp.ones((LANES, LANES), f32), 1)
    mspec = functools.partial(_mod_spec, tiles_per_b=tiles_per_b, n_batch=n_batch)
    tab_specs = [_table_spec(tiles_per_b)] * 3
    route_in, route_out = _route_specs(d, tiles_per_b, n_batch)
    route_shapes = _route_out_shapes(t_all, d)
    route_scratch = [pltpu.VMEM((1, LANES), f32)]

    for i in range(depth):
        kind, j = i % N_MIXERS, i // N_MIXERS
        m3 = mods[i]
        gmix = norm_mix_g[i][None]
        head_in = [_tile_spec(d), mspec(0), mspec(1), _row_spec(d)]
        wr = jnp.pad(jnp.concatenate([moe_w_group[i], moe_w_expert[i]], axis=1),
                     ((0, 0), (0, LANES - MOE_GROUPS - MOE_EXPERTS)))
        br = jnp.pad(jnp.concatenate([moe_b_group[i], moe_b_expert[i]]),
                     (0, LANES - MOE_GROUPS - MOE_EXPERTS))[None]
        route_args = (m3, norm_ffn_g[i][None], m3, m3, wr, br, ltri, upper)

        if kind == 0:
            dk, dv = 2 * LANES, MLA_V
            wcat = jnp.pad(jnp.concatenate([mla_w_dq[j], mla_w_dkv[j]], axis=1),
                           ((0, 0), (0, LANES - MLA_ROPE))).astype(bf16)
            wuq = jnp.pad(mla_w_uq[j].reshape(MLA_Q_LORA, MLA_HEADS, MLA_NOPE + MLA_ROPE),
                          ((0, 0), (0, 0), (0, dk - MLA_NOPE - MLA_ROPE))
                          ).reshape(MLA_Q_LORA, MLA_HEADS * dk).astype(bf16)
            wukv3 = mla_w_ukv[j].reshape(MLA_KV_LORA, MLA_HEADS, MLA_NOPE + MLA_V)
            wukv = jnp.concatenate([wukv3[..., :MLA_NOPE].reshape(MLA_KV_LORA, -1),
                                    wukv3[..., MLA_NOPE:].reshape(MLA_KV_LORA, -1)], axis=1).astype(bf16)
            q, k, v = pl.pallas_call(
                functools.partial(_mla_proj_body, scale=(MLA_NOPE + MLA_ROPE) ** -0.5 * LOG2E),
                grid=(n_tiles,),
                in_specs=head_in + [_full_spec(wcat.shape), _row_spec(MLA_Q_LORA), _row_spec(MLA_KV_LORA),
                                    _full_spec(wuq.shape), _full_spec(wukv.shape)] + tab_specs,
                out_specs=[_tile_spec(MLA_HEADS * dk), _tile_spec(MLA_HEADS * dk), _tile_spec(MLA_HEADS * dv)],
                out_shape=[jax.ShapeDtypeStruct((t_all, MLA_HEADS * dk), bf16),
                           jax.ShapeDtypeStruct((t_all, MLA_HEADS * dk), bf16),
                           jax.ShapeDtypeStruct((t_all, MLA_HEADS * dv), bf16)],
                compiler_params=_params("arbitrary"),
                name="mla_proj",
            )(xs, m3, m3, gmix, wcat, mla_g_q[j][None], mla_g_kv[j][None], wuq, wukv, *mla_tabs)
            att = _attn_call(q, k, v, n_batch, MLA_HEADS, MLA_HEADS, dk, dv, n_ctx)
            w_o = mla_w_o[j].astype(bf16)
        elif kind == 2:
            dk = dv = GQA_HEAD_DIM
            wqkv = gqa_w_qkv[j].astype(bf16)
            q, k, v = pl.pallas_call(
                functools.partial(_gqa_proj_body, scale=GQA_HEAD_DIM ** -0.5 * LOG2E),
                grid=(n_tiles,),
                in_specs=head_in + [_full_spec(wqkv.shape), _row_spec(dk), _row_spec(dk)] + tab_specs,
                out_specs=[_tile_spec(GQA_HEADS * dk), _tile_spec(GQA_KV_HEADS * dk),
                           _tile_spec(GQA_KV_HEADS * dv)],
                out_shape=[jax.ShapeDtypeStruct((t_all, GQA_HEADS * dk), bf16),
                           jax.ShapeDtypeStruct((t_all, GQA_KV_HEADS * dk), bf16),
                           jax.ShapeDtypeStruct((t_all, GQA_KV_HEADS * dv), bf16)],
                compiler_params=_params("arbitrary"),
                name="gqa_proj",
            )(xs, m3, m3, gmix, wqkv, gqa_q_norm_g[j][None], gqa_k_norm_g[j][None], *gqa_tabs)
            att = _attn_call(q, k, v, n_batch, GQA_HEADS, GQA_KV_HEADS, dk, dv, n_ctx)
            w_o = gqa_w_o[j].astype(bf16)

        if kind == 1:
            w_in = conv_w_in[j].astype(bf16)
            bg, cu = pl.pallas_call(
                _conv_proj_body,
                grid=(n_tiles,),
                in_specs=head_in + [_full_spec(w_in.shape)],
                out_specs=[_tile_spec(d), _tile_spec(d)],
                out_shape=[jax.ShapeDtypeStruct((t_all, d), bf16)] * 2,
                compiler_params=_params("arbitrary"),
                name="conv_proj",
            )(xs, m3, m3, gmix, w_in)
            w_o = conv_w_out[j].astype(bf16)
            per = TM // HALO
            last = t_all // HALO - 1
            xs, f, meta, wts, runs, cnt = pl.pallas_call(
                functools.partial(_conv_out_body, tiles_per_b=tiles_per_b),
                grid=(n_tiles,),
                in_specs=[_tile_spec(d),
                          pl.BlockSpec((HALO, d), lambda i: (jnp.maximum(i * per - 1, 0), 0)),
                          pl.BlockSpec((HALO, d), lambda i: (jnp.minimum((i + 1) * per, last), 0)),
                          _tile_spec(d), _full_spec((3, d)), _full_spec(w_o.shape), _tile_spec(d)] + route_in,
                out_specs=route_out,
                out_shape=route_shapes,
                scratch_shapes=route_scratch,
                compiler_params=_params("arbitrary"),
                name="conv_out_route",
            )(cu, cu, cu, bg, conv_w[j], w_o, xs, *route_args)
        else:
            xs, f, meta, wts, runs, cnt = pl.pallas_call(
                _attn_out_body,
                grid=(n_tiles,),
                in_specs=[_tile_spec(att.shape[-1]), _full_spec(w_o.shape), _tile_spec(d)] + route_in,
                out_specs=route_out,
                out_shape=route_shapes,
                scratch_shapes=route_scratch,
                compiler_params=_params("arbitrary"),
                name="attn_out_route",
            )(att, w_o, xs, *route_args)

        counts = cnt[0, ROUTE_LANE0:ROUTE_LANE0 + MOE_EXPERTS].astype(jnp.int32)
        padded = (counts + RUN_ALIGN - 1 + EXPERT_BLOCK - 1) // EXPERT_BLOCK * EXPERT_BLOCK
        pends = jnp.cumsum(padded)
        pstarts = pends - padded
        blk_row0 = jnp.arange(n_blocks, dtype=jnp.int32) * EXPERT_BLOCK
        blk_e = jnp.minimum(jnp.sum((pends[None, :] <= blk_row0[:, None]).astype(jnp.int32), axis=1),
                            MOE_EXPERTS - 1)
        n_act = (pends[-1:] // EXPERT_BLOCK).astype(jnp.int32)
        seg = jnp.concatenate([pstarts, pstarts + counts, padded - counts, n_act,
                               jnp.zeros((MOE_EXPERTS - 1,), jnp.int32)]).astype(jnp.int32)

        xp = _dispatch_call(seg, runs, meta, f, n_rows)
        yp = _expert_call(blk_e, n_act, xp, moe_w_gate, moe_w_up, moe_w_down, i)
        final = i == depth - 1
        xs = _combine_call(seg, runs, xs, wts, m3, final_norm_g[None], yp, n_batch, tiles_per_b, final)

    return xs.reshape(n_batch, seq, d)
```

```python
import functools

import jax
import jax.numpy as jnp
from jax import lax
from jax.experimental import pallas as pl
from jax.experimental.pallas import tpu as pltpu

f32 = jnp.float32
bf16 = jnp.bfloat16

GRID_W = 64
ROPE_THETA = 10000.0
NORM_EPS = 1e-6
N_MOD = 6
N_MIXERS = 3
MLA_HEADS, MLA_Q_LORA, MLA_KV_LORA, MLA_NOPE, MLA_ROPE, MLA_V = 8, 512, 256, 128, 64, 128
GQA_HEADS, GQA_KV_HEADS, GQA_HEAD_DIM = 16, 8, 128
MOE_GROUPS, MOE_EXPERTS_PER_GROUP, MOE_TOP_K = 4, 8, 2
MOE_EXPERTS = MOE_GROUPS * MOE_EXPERTS_PER_GROUP

LANES = 128
TM = 256
EXPERT_BLOCK = 256
SUBLANES = 8
RUN_ALIGN = 8
PAD_PIECES = tuple(1 << b for b in reversed(range(EXPERT_BLOCK.bit_length())))
HALO = 16
ATTN_KEY_CHUNK = 256
ATTN_HEADS_PER_STEP = 4
LOG2E = 1.4426950408889634
VMEM_LIMIT = 56 * 1024 * 1024
ROUTE_LANE0 = MOE_GROUPS
META_ROWS = 8
MXU_TILE = 256
LOCAL_ROWS = -(-(TM * MOE_TOP_K + MOE_EXPERTS * (RUN_ALIGN - 1)) // MXU_TILE) * MXU_TILE
NEG = -1e30


def _params(*sem):
    return pltpu.CompilerParams(dimension_semantics=sem, vmem_limit_bytes=VMEM_LIMIT)


def _rms(x):
    return x * lax.rsqrt(jnp.mean(x * x, axis=-1, keepdims=True) + NORM_EPS)


def _modulate(x, gn, sh, sc):
    return _rms(x) * gn * (1.0 + sc) + sh


def _silu(x):
    return x / (1.0 + jnp.exp(-x))


def _rope(x, c, s1, s2, shift):
    return x * c + pltpu.roll(x, LANES - shift, 1) * s1 + pltpu.roll(x, shift, 1) * s2


def _mod_body(cond_ref, w_ref, b_ref, o_ref):
    a = _silu(cond_ref[...])
    o_ref[...] = jnp.dot(a, w_ref[...], preferred_element_type=f32,
                         precision=lax.Precision.HIGHEST) + b_ref[...]


def _mod_call(cond8, w_mod, b_mod):
    depth, d, n6 = w_mod.shape
    tn = n6 // 4
    return pl.pallas_call(
        _mod_body,
        grid=(depth, n6 // tn),
        in_specs=[pl.BlockSpec((8, d), lambda l, j: (0, 0)),
                  pl.BlockSpec((None, d, tn), lambda l, j: (l, 0, j)),
                  pl.BlockSpec((None, 1, tn), lambda l, j: (l, 0, j))],
        out_specs=pl.BlockSpec((None, 8, tn), lambda l, j: (l, 0, j)),
        out_shape=jax.ShapeDtypeStruct((depth, 8, n6), f32),
        compiler_params=_params("arbitrary", "arbitrary"),
        name="mod_vectors",
    )(cond8, w_mod, b_mod.reshape(depth, 1, n6))


def _mod_spec(chunk, tiles_per_b, n_batch):
    def imap(i):
        r = jnp.where(i % tiles_per_b == 0, n_batch, i // tiles_per_b)
        return (r * N_MOD + chunk, 0, 0)
    return pl.BlockSpec((None, 1, 1024), imap)


def _row_spec(n):
    return pl.BlockSpec((1, n), lambda i: (0, 0))


def _full_spec(shape):
    return pl.BlockSpec(shape, lambda i: (0,) * len(shape))


def _tile_spec(n):
    return pl.BlockSpec((TM, n), lambda i: (i, 0))


def _mla_proj_body(x_ref, sh_ref, sc_ref, gn_ref, wcat_ref, gq_ref, gkv_ref, wuq_ref, wukv_ref,
                   c_ref, s1_ref, s2_ref, q_ref, k_ref, v_ref, *, scale):
    a = _modulate(x_ref[...], gn_ref[...], sh_ref[...], sc_ref[...]).astype(bf16)
    z = jnp.dot(a, wcat_ref[...], preferred_element_type=f32)
    cq = (_rms(z[:, :MLA_Q_LORA]) * gq_ref[...]).astype(bf16)
    ckv = (_rms(z[:, MLA_Q_LORA:MLA_Q_LORA + MLA_KV_LORA]) * gkv_ref[...]).astype(bf16)
    kpe = z[:, MLA_Q_LORA + MLA_KV_LORA:]
    q = jnp.dot(cq, wuq_ref[...], preferred_element_type=f32)
    kv = jnp.dot(ckv, wukv_ref[...], preferred_element_type=f32)
    c, s1, s2 = c_ref[...], s1_ref[...], s2_ref[...]
    half = MLA_ROPE // 4
    kpe_r = _rope(kpe, c, s1, s2, half).astype(bf16)
    hd = 2 * LANES
    for h in range(MLA_HEADS):
        q_ref[:, h * hd:h * hd + LANES] = (q[:, h * hd:h * hd + LANES] * scale).astype(bf16)
        q_ref[:, h * hd + LANES:(h + 1) * hd] = (
            _rope(q[:, h * hd + LANES:(h + 1) * hd], c, s1, s2, half) * scale).astype(bf16)
        k_ref[:, h * hd:h * hd + LANES] = kv[:, h * LANES:(h + 1) * LANES].astype(bf16)
        k_ref[:, h * hd + LANES:(h + 1) * hd] = kpe_r
    v_ref[...] = kv[:, MLA_HEADS * MLA_NOPE:].astype(bf16)


def _gqa_proj_body(x_ref, sh_ref, sc_ref, gn_ref, w_ref, qg_ref, kg_ref,
                   c_ref, s1_ref, s2_ref, q_ref, k_ref, v_ref, *, scale):
    a = _modulate(x_ref[...], gn_ref[...], sh_ref[...], sc_ref[...]).astype(bf16)
    z = jnp.dot(a, w_ref[...], preferred_element_type=f32)
    c, s1, s2 = c_ref[...], s1_ref[...], s2_ref[...]
    half = GQA_HEAD_DIM // 4
    dq = GQA_HEADS * GQA_HEAD_DIM
    dkv = GQA_KV_HEADS * GQA_HEAD_DIM
    qg = qg_ref[...] * scale
    kg = kg_ref[...]
    for h in range(GQA_HEADS):
        qh = _rms(z[:, h * LANES:(h + 1) * LANES]) * qg
        q_ref[:, h * LANES:(h + 1) * LANES] = _rope(qh, c, s1, s2, half).astype(bf16)
    for h in range(GQA_KV_HEADS):
        kh = _rms(z[:, dq + h * LANES:dq + (h + 1) * LANES]) * kg
        k_ref[:, h * LANES:(h + 1) * LANES] = _rope(kh, c, s1, s2, half).astype(bf16)
    v_ref[...] = z[:, dq + dkv:].astype(bf16)


def _conv_proj_body(x_ref, sh_ref, sc_ref, gn_ref, w_ref, bg_ref, cu_ref):
    a = _modulate(x_ref[...], gn_ref[...], sh_ref[...], sc_ref[...]).astype(bf16)
    z = jnp.dot(a, w_ref[...], preferred_element_type=f32)
    d = bg_ref.shape[-1]
    bg_ref[...] = z[:, :d].astype(bf16)
    cu_ref[...] = (z[:, d:2 * d] * z[:, 2 * d:]).astype(bf16)


def _lane_groups(x, op):
    parts = [x[:, i * LANES:(i + 1) * LANES] for i in range(x.shape[1] // LANES)]
    while len(parts) > 1:
        parts = [op(parts[i], parts[i + 1]) if i + 1 < len(parts) else parts[i]
                 for i in range(0, len(parts), 2)]
    return parts[0]


def _attn_body(q_ref, k_ref, v_ref, o_ref, s_scr, *, n_ctx, n_all, heads_per_step, group, dk, dv):
    def run(nk):
        chunks = [(a, min(a + ATTN_KEY_CHUNK, nk)) for a in range(0, nk, ATTN_KEY_CHUNK)]
        for j in range(heads_per_step):
            kv = j // group
            q = q_ref[:, j * dk:(j + 1) * dk]
            m = None
            for a, b in chunks:
                s = lax.dot_general(q, k_ref[a:b, kv * dk:(kv + 1) * dk], (((1,), (1,)), ((), ())),
                                    preferred_element_type=f32)
                s_scr[j, :, a:b] = s
                mc = _lane_groups(s, jnp.maximum)
                m = mc if m is None else jnp.maximum(m, mc)
            m = jnp.max(m, axis=-1, keepdims=True)
            o = l = None
            for a, b in chunks:
                p = jnp.exp2(s_scr[j, :, a:b] - m)
                lc = _lane_groups(p, jnp.add)
                oc = jnp.dot(p.astype(bf16), v_ref[a:b, kv * dv:(kv + 1) * dv], preferred_element_type=f32)
                o, l = (oc, lc) if o is None else (o + oc, l + lc)
            l = jnp.sum(l, axis=-1, keepdims=True)
            o_ref[:, j * dv:(j + 1) * dv] = (o / l).astype(o_ref.dtype)

    i = pl.program_id(2)
    pl.when(i == 0)(lambda: run(n_ctx))
    pl.when(i > 0)(lambda: run(n_all))


def _attn_call(q, k, v, n_batch, heads, kv_heads, dk, dv, n_ctx):
    nt = q.shape[0] // n_batch
    g = heads // kv_heads
    hp = ATTN_HEADS_PER_STEP
    kh = hp // g
    q3, k3, v3 = (a.reshape(n_batch, nt, a.shape[-1]) for a in (q, k, v))
    out = pl.pallas_call(
        functools.partial(_attn_body, n_ctx=n_ctx, n_all=nt, heads_per_step=hp, group=g, dk=dk, dv=dv),
        grid=(n_batch, heads // hp, nt // TM),
        in_specs=[pl.BlockSpec((None, TM, hp * dk), lambda b, h, i: (b, i, h)),
                  pl.BlockSpec((None, nt, kh * dk), lambda b, h, i: (b, 0, h)),
                  pl.BlockSpec((None, nt, kh * dv), lambda b, h, i: (b, 0, h))],
        out_specs=pl.BlockSpec((None, TM, hp * dv), lambda b, h, i: (b, i, h)),
        out_shape=jax.ShapeDtypeStruct((n_batch, nt, heads * dv), bf16),
        scratch_shapes=[pltpu.VMEM((hp, TM, nt), f32)],
        compiler_params=_params("arbitrary", "arbitrary", "arbitrary"),
        name="attention",
    )(q3, k3, v3)
    return out.reshape(n_batch * nt, heads * dv)


def _route_epilogue(x, y, g1_ref, gn2_ref, sh2_ref, sc2_ref, wr_ref, br_ref, ltri_ref, upper_ref,
                    xo_ref, f_ref, meta_ref, wts_ref, runs_ref, cnt_ref, carry_ref):
    i = pl.program_id(0)

    @pl.when(i == 0)
    def _():
        carry_ref[...] = jnp.zeros_like(carry_ref)

    xn = x + g1_ref[...] * y
    xo_ref[...] = xn
    f = _modulate(xn, gn2_ref[...], sh2_ref[...], sc2_ref[...])
    f_hi = f.astype(bf16)
    f_ref[...] = f_hi
    f_lo = (f - f_hi.astype(f32)).astype(bf16)
    logits = (jnp.dot(f_hi, wr_ref[0], preferred_element_type=f32)
              + jnp.dot(f_lo, wr_ref[0], preferred_element_type=f32)
              + jnp.dot(f_hi, wr_ref[1], preferred_element_type=f32))
    lane = lax.broadcasted_iota(jnp.int32, logits.shape, 1).astype(f32)
    bias = br_ref[...]

    def first_argmax(score):
        top = jnp.max(score, axis=-1, keepdims=True)
        return jnp.min(jnp.where(score == top, lane, float(LANES)), axis=-1, keepdims=True)

    def masked_softmax(mask):
        z = jnp.where(mask, logits, NEG)
        e = jnp.exp(z - jnp.max(z, axis=-1, keepdims=True))
        return e / jnp.sum(e, axis=-1, keepdims=True)

    def pick(onehot, v):
        return jnp.sum(jnp.where(onehot, v, 0.0), axis=-1, keepdims=True)

    gmask = lane < float(MOE_GROUPS)
    gp = masked_softmax(gmask)
    g_sel = first_argmax(jnp.where(gmask, gp + bias, NEG))
    g_w = pick(lane == g_sel, gp)
    lo = float(ROUTE_LANE0) + float(MOE_EXPERTS_PER_GROUP) * g_sel
    emask = (lane >= lo) & (lane < lo + float(MOE_EXPERTS_PER_GROUP))
    ep = masked_softmax(emask)
    escore = jnp.where(emask, ep + bias, NEG)
    i1 = first_argmax(escore)
    oh1 = lane == i1
    i2 = first_argmax(jnp.where(oh1, NEG, escore))
    oh2 = lane == i2
    p1, p2 = pick(oh1, ep), pick(oh2, ep)
    w1 = p1 / (p1 + p2) * g_w
    w2 = p2 / (p1 + p2) * g_w

    hits = jnp.where(oh1 | oh2, 1.0, 0.0)
    before = jnp.dot(ltri_ref[...], hits.astype(bf16), preferred_element_type=f32)
    n_tile = jnp.sum(hits, axis=0, keepdims=True)
    n_al = jnp.floor((n_tile + float(RUN_ALIGN - 1)) * (1.0 / RUN_ALIGN)) * float(RUN_ALIGN)
    lo = jnp.dot(jnp.broadcast_to(n_al, (SUBLANES, LANES)), upper_ref[...], preferred_element_type=f32,
                 precision=lax.Precision.HIGHEST)[0:1, :]
    lp1, lp2 = pick(oh1, before + lo), pick(oh2, before + lo)

    row = lax.broadcasted_iota(jnp.int32, runs_ref.shape, 0)
    runs = jnp.where(row == 0, n_tile, jnp.where(row == 1, carry_ref[...], jnp.where(row == 2, lo, 0.0)))
    runs_ref[...] = runs.astype(jnp.int32)
    carry_ref[...] = carry_ref[...] + n_tile
    cnt_ref[...] = carry_ref[...]

    meta = jnp.where(lane == 0.0, lp1, jnp.where(lane == 1.0, lp2, 0.0))
    meta_ref[...] = meta.T[:meta_ref.shape[0], :].astype(jnp.int32)
    col = lax.broadcasted_iota(jnp.int32, wts_ref.shape, 1)
    wts_ref[...] = jnp.where(col == 0, w1, jnp.where(col == 1, w2, jnp.where(col == 2, lp1, lp2)))


def _attn_out_body(y_ref, wo_ref, x_ref, *rest):
    y = jnp.dot(y_ref[...], wo_ref[...], preferred_element_type=f32)
    _route_epilogue(x_ref[...], y, *rest)


def _conv_out_body(cu_ref, prev_ref, next_ref, bg_ref, cw_ref, wo_ref, x_ref, *rest, tiles_per_b):
    t = pl.program_id(0) % tiles_per_b
    has_prev = (t >= 2).astype(f32)
    has_next = ((t != 0) & (t != tiles_per_b - 1)).astype(f32)
    cu = cu_ref[...].astype(f32)
    row = lax.broadcasted_iota(jnp.int32, cu.shape, 0)
    hp = prev_ref[HALO - 1:HALO, :].astype(f32) * has_prev
    hn = next_ref[0:1, :].astype(f32) * has_next
    up = jnp.where(row == 0, hp, pltpu.roll(cu, 1, 0))
    dn = jnp.where(row == TM - 1, hn, pltpu.roll(cu, TM - 1, 0))
    cw = cw_ref[...]
    z = cw[0:1, :] * up + cw[1:2, :] * cu + cw[2:3, :] * dn
    yb = (bg_ref[...].astype(f32) * z).astype(bf16)
    y = jnp.dot(yb, wo_ref[...], preferred_element_type=f32)
    _route_epilogue(x_ref[...], y, *rest)


def _route_specs(d, tiles_per_b, n_batch):
    ins = [_mod_spec(2, tiles_per_b, n_batch), _row_spec(d), _mod_spec(3, tiles_per_b, n_batch),
           _mod_spec(4, tiles_per_b, n_batch), _full_spec((2, d, LANES)), _row_spec(LANES),
           _full_spec((TM, TM)), _full_spec((LANES, LANES))]
    outs = [_tile_spec(d), _tile_spec(d), pl.BlockSpec((None, META_ROWS, TM), lambda i: (i, 0, 0)),
            _tile_spec(8), pl.BlockSpec((None, SUBLANES, LANES), lambda i: (i, 0, 0)), _row_spec(LANES)]
    return ins, outs


def _route_out_shapes(t, d):
    return [jax.ShapeDtypeStruct((t, d), f32), jax.ShapeDtypeStruct((t, d), bf16),
            jax.ShapeDtypeStruct((t // TM, META_ROWS, TM), jnp.int32), jax.ShapeDtypeStruct((t, 8), f32),
            jax.ShapeDtypeStruct((t // TM, SUBLANES, LANES), jnp.int32),
            jax.ShapeDtypeStruct((1, LANES), f32)]


def _rows_to_tiles(tref, x):
    n = x.shape[0]
    for s in range(SUBLANES):
        tref[pl.ds(s, n, stride=SUBLANES), :] = x[:, s * LANES:(s + 1) * LANES]


def _tiles_to_rows(tref):
    n = tref.shape[0] // SUBLANES
    return jnp.concatenate([tref[pl.ds(s, n, stride=SUBLANES), :] for s in range(SUBLANES)], axis=-1)


def _tile_rows(ref, row, n):
    return ref.at[pl.ds(row * SUBLANES, n * SUBLANES)]


def _run_copies(seg_ref, runs_ref, copy):
    def per_expert(e, total):
        lane = ROUTE_LANE0 + e
        n_chunks = (runs_ref[0, lane] + RUN_ALIGN - 1) // RUN_ALIGN
        slot0 = seg_ref[e] + runs_ref[1, lane]
        local0 = runs_ref[2, lane]

        def chunk(c, carry):
            copy(local0 + c * RUN_ALIGN, slot0 + c * RUN_ALIGN).start()
            return carry

        lax.fori_loop(0, n_chunks, chunk, 0)
        return total + n_chunks

    return lax.fori_loop(0, MOE_EXPERTS, per_expert, 0)


def _wait_chunks(n_chunks, copy):
    def wait_one(c, carry):
        copy(0, 0).wait()
        return carry
    lax.fori_loop(0, n_chunks, wait_one, 0)


def _dispatch_body(seg_ref, runs_ref, meta_ref, f_ref, xp_ref, sbuf, zbuf, pending, sem, zsem):
    i = pl.program_id(0)
    n_blocks = xp_ref.shape[0] // (EXPERT_BLOCK * SUBLANES)

    @pl.when(i == 0)
    def _():
        zbuf[...] = jnp.zeros_like(zbuf)

        def pad_fill(e, wait):
            n_pad = seg_ref[2 * MOE_EXPERTS + e]
            off = seg_ref[MOE_EXPERTS + e]
            for piece in PAD_PIECES:
                take = (n_pad & piece) != 0
                cp = pltpu.make_async_copy(_tile_rows(zbuf, 0, piece), _tile_rows(xp_ref, off, piece), zsem)
                pl.when(take)(cp.wait if wait else cp.start)
                off = off + jnp.where(take, piece, 0)

        def tail_fill(b, wait):
            cp = pltpu.make_async_copy(zbuf, _tile_rows(xp_ref, b * EXPERT_BLOCK, EXPERT_BLOCK), zsem)
            cp.wait() if wait else cp.start()

        for wait in (False, True):
            lax.fori_loop(0, MOE_EXPERTS, lambda e, c, w=wait: (pad_fill(e, w), c)[1], 0)
            lax.fori_loop(seg_ref[3 * MOE_EXPERTS], n_blocks, lambda b, c, w=wait: (tail_fill(b, w), c)[1], 0)

        pending[0] = 0

    buf = sbuf.at[i % 2]
    pos = lax.broadcasted_iota(jnp.int32, (LOCAL_ROWS, TM), 0)
    m = meta_ref[...]
    perm = jnp.where((pos == m[0:1, :]) | (pos == m[1:2, :]), 1.0, 0.0).astype(bf16)
    _rows_to_tiles(buf, jnp.dot(perm, f_ref[...], preferred_element_type=f32))

    def copy(local_row, slot_row):
        return pltpu.make_async_copy(_tile_rows(buf, local_row, RUN_ALIGN),
                                     _tile_rows(xp_ref, slot_row, RUN_ALIGN), sem)

    _wait_chunks(pending[0], copy)
    pending[0] = _run_copies(seg_ref, runs_ref, copy)

    @pl.when(i == pl.num_programs(0) - 1)
    def _():
        _wait_chunks(pending[0], copy)


def _dispatch_call(seg, runs, meta, f, n_rows):
    t, d = f.shape
    grid_spec = pltpu.PrefetchScalarGridSpec(
        num_scalar_prefetch=1,
        grid=(t // TM,),
        in_specs=[pl.BlockSpec((None, SUBLANES, LANES), lambda i, seg: (i, 0, 0), memory_space=pltpu.SMEM),
                  pl.BlockSpec((None, META_ROWS, TM), lambda i, seg: (i, 0, 0)),
                  pl.BlockSpec((TM, d), lambda i, seg: (i, 0))],
        out_specs=pl.BlockSpec(memory_space=pl.ANY),
        scratch_shapes=[pltpu.VMEM((2, LOCAL_ROWS * SUBLANES, LANES), f32),
                        pltpu.VMEM((EXPERT_BLOCK * SUBLANES, LANES), f32),
                        pltpu.SMEM((1,), jnp.int32),
                        pltpu.SemaphoreType.DMA(()), pltpu.SemaphoreType.DMA(())],
    )
    return pl.pallas_call(
        _dispatch_body,
        grid_spec=grid_spec,
        out_shape=jax.ShapeDtypeStruct((n_rows * SUBLANES, LANES), f32),
        compiler_params=_params("arbitrary"),
        name="moe_dispatch",
    )(seg, runs, meta, f)


def _expert_body(be_ref, na_ref, x_ref, wg_hbm, wu_hbm, wd_hbm, o_ref,
                 wg_f, wu_f, wd_f, wg_s, wu_s, wd_s, wsem, *, layer):
    i = pl.program_id(0)
    active = i < na_ref[0]
    e = be_ref[i]
    changed = (i == 0) | (e != be_ref[jnp.maximum(i - 1, 0)])

    def fetch(expert):
        pairs = ((wg_hbm, wg_f), (wu_hbm, wu_f), (wd_hbm, wd_f))
        return [pltpu.make_async_copy(w.at[layer, expert], buf, wsem.at[j]) for j, (w, buf) in enumerate(pairs)]

    @pl.when(i == 0)
    def _():
        for cp in fetch(e):
            cp.start()

    @pl.when(active & changed)
    def _():
        for cp in fetch(e):
            cp.wait()
        wg_s[...] = wg_f[...].astype(bf16)
        wu_s[...] = wu_f[...].astype(bf16)
        wd_s[...] = wd_f[...].astype(bf16)

        @pl.when(e + 1 < MOE_EXPERTS)
        def _():
            for cp in fetch(e + 1):
                cp.start()

    @pl.when(active)
    def _():
        x = _tiles_to_rows(x_ref).astype(bf16)
        g = jnp.dot(x, wg_s[...], preferred_element_type=f32)
        u = jnp.dot(x, wu_s[...], preferred_element_type=f32)
        h = (_silu(g) * u).astype(bf16)
        _rows_to_tiles(o_ref, jnp.dot(h, wd_s[...], preferred_element_type=f32))

    @pl.when(jnp.logical_not(active))
    def _():
        o_ref[...] = jnp.zeros_like(o_ref)


def _expert_call(blk_e, n_act, xp, w_gate, w_up, w_down, layer):
    n_rows = xp.shape[0] // SUBLANES
    d, ff = w_gate.shape[-2:]
    blk_rows = EXPERT_BLOCK * SUBLANES

    def blk(i, be, na):
        return jnp.maximum(jnp.minimum(i, na[0] - 1), 0)

    any_spec = pl.BlockSpec(memory_space=pl.ANY)
    grid_spec = pltpu.PrefetchScalarGridSpec(
        num_scalar_prefetch=2,
        grid=(n_rows // EXPERT_BLOCK,),
        in_specs=[pl.BlockSpec((blk_rows, LANES), lambda i, be, na: (blk(i, be, na), 0)),
                  any_spec, any_spec, any_spec],
        out_specs=pl.BlockSpec((blk_rows, LANES), lambda i, be, na: (i, 0)),
        scratch_shapes=[pltpu.VMEM((d, ff), f32), pltpu.VMEM((d, ff), f32), pltpu.VMEM((ff, d), f32),
                        pltpu.VMEM((d, ff), bf16), pltpu.VMEM((d, ff), bf16), pltpu.VMEM((ff, d), bf16),
                        pltpu.SemaphoreType.DMA((3,))],
    )
    return pl.pallas_call(
        functools.partial(_expert_body, layer=layer),
        grid_spec=grid_spec,
        out_shape=jax.ShapeDtypeStruct((n_rows * SUBLANES, LANES), f32),
        compiler_params=_params("arbitrary"),
        name="moe_experts",
    )(blk_e, n_act, xp, w_gate, w_up, w_down)


def _combine_body(seg_ref, runs_ref, next_runs_ref, x_ref, wts_ref, g2_ref, fg_ref, yp_ref, o_ref,
                  gbuf, pending, sem, *, final):
    step = pl.program_id(0) * pl.num_programs(1) + pl.program_id(1)
    n_steps = pl.num_programs(0) * pl.num_programs(1)
    cur = step % 2

    def fetch(buf):
        def copy(local_row, slot_row):
            return pltpu.make_async_copy(_tile_rows(yp_ref, slot_row, RUN_ALIGN),
                                         _tile_rows(gbuf.at[buf], local_row, RUN_ALIGN), sem.at[buf])
        return copy

    @pl.when(step == 0)
    def _():
        gbuf[...] = jnp.zeros_like(gbuf)
        pending[0] = _run_copies(seg_ref, runs_ref, fetch(0))

    @pl.when(step + 1 < n_steps)
    def _():
        pending[1 - cur] = _run_copies(seg_ref, next_runs_ref, fetch(1 - cur))

    _wait_chunks(pending[cur], fetch(cur))
    ys = _tiles_to_rows(gbuf.at[cur]).astype(bf16)
    w = wts_ref[...]
    pos = lax.broadcasted_iota(jnp.int32, (TM, LOCAL_ROWS), 1).astype(f32)
    y = None
    for k in range(MOE_TOP_K):
        pick = jnp.where(pos == w[:, MOE_TOP_K + k:MOE_TOP_K + k + 1], 1.0, 0.0).astype(bf16)
        yk = w[:, k:k + 1] * jnp.dot(pick, ys, preferred_element_type=f32)
        y = yk if y is None else y + yk
    xn = x_ref[...] + g2_ref[...] * y
    if final:
        xn = _rms(xn) * fg_ref[...]
    o_ref[...] = xn


def _combine_call(seg, runs, x, wts, mods3, final_g, yp, n_batch, tiles_per_b, final):
    t, d = x.shape
    t0 = 1 if final else 0
    nt = tiles_per_b - t0

    def tile(b, j):
        return b * tiles_per_b + t0 + j

    def next_tile(b, j):
        nxt = jnp.minimum(b * nt + j + 1, n_batch * nt - 1)
        return tile(nxt // nt, nxt % nt)

    def g2_map(b, j, seg):
        r = jnp.where(t0 + j == 0, n_batch, b)
        return (r * N_MOD + 5, 0, 0)

    grid_spec = pltpu.PrefetchScalarGridSpec(
        num_scalar_prefetch=1,
        grid=(n_batch, nt),
        in_specs=[pl.BlockSpec((None, SUBLANES, LANES), lambda b, j, seg: (tile(b, j), 0, 0),
                               memory_space=pltpu.SMEM),
                  pl.BlockSpec((None, SUBLANES, LANES), lambda b, j, seg: (next_tile(b, j), 0, 0),
                               memory_space=pltpu.SMEM),
                  pl.BlockSpec((TM, d), lambda b, j, seg: (tile(b, j), 0)),
                  pl.BlockSpec((TM, 8), lambda b, j, seg: (tile(b, j), 0)),
                  pl.BlockSpec((None, 1, d), g2_map),
                  pl.BlockSpec((1, d), lambda b, j, seg: (0, 0)),
                  pl.BlockSpec(memory_space=pl.ANY)],
        out_specs=pl.BlockSpec((TM, d), lambda b, j, seg: (b * nt + j, 0)),
        scratch_shapes=[pltpu.VMEM((2, LOCAL_ROWS * SUBLANES, LANES), f32), pltpu.SMEM((2,), jnp.int32),
                        pltpu.SemaphoreType.DMA((2,))],
    )
    return pl.pallas_call(
        functools.partial(_combine_body, final=final),
        grid_spec=grid_spec,
        out_shape=jax.ShapeDtypeStruct((n_batch * nt * TM, d), f32),
        compiler_params=_params("arbitrary", "arbitrary"),
        name="moe_combine",
    )(seg, runs, runs, x, wts, mods3, final_g, yp)


def _rope_tables(seq, n_ctx, half):
    inv = ROPE_THETA ** (-jnp.arange(half, dtype=f32) / half)
    t = jnp.arange(seq, dtype=jnp.int32)
    ang_r = (t // GRID_W).astype(f32)[:, None] * inv[None, :]
    ang_c = (t % GRID_W).astype(f32)[:, None] * inv[None, :]
    zero = jnp.zeros_like(ang_r)
    cos = jnp.concatenate([jnp.cos(ang_r), jnp.cos(ang_r), jnp.cos(ang_c), jnp.cos(ang_c)], axis=-1)
    s1 = jnp.concatenate([-jnp.sin(ang_r), zero, -jnp.sin(ang_c), zero], axis=-1)
    s2 = jnp.concatenate([zero, jnp.sin(ang_r), zero, jnp.sin(ang_c)], axis=-1)
    w = 4 * half
    ctx_c = jnp.ones((n_ctx, w), f32)
    ctx_s = jnp.zeros((n_ctx, w), f32)
    pad = ((0, 0), (0, LANES - w))
    return (jnp.pad(jnp.concatenate([ctx_c, cos], axis=0), pad),
            jnp.pad(jnp.concatenate([ctx_s, s1], axis=0), pad),
            jnp.pad(jnp.concatenate([ctx_s, s2], axis=0), pad))


def _table_spec(tiles_per_b):
    return pl.BlockSpec((TM, LANES), lambda i: (i % tiles_per_b, 0))


def kernel(x, c, ctx, c_ctx, w_mod, b_mod, norm_mix_g, norm_ffn_g, mla_w_dq, mla_g_q, mla_w_uq, mla_w_dkv, mla_g_kv, mla_w_ukv, mla_w_o, conv_w_in, conv_w, conv_w_out, gqa_w_qkv, gqa_q_norm_g, gqa_k_norm_g, gqa_w_o, moe_w_group, moe_b_group, moe_w_expert, moe_b_expert, moe_w_gate, moe_w_up, moe_w_down, final_norm_g):
    n_batch, seq, d = x.shape
    n_ctx = ctx.shape[1]
    depth = w_mod.shape[0]
    assert n_ctx == TM and seq % TM == 0 and d == 1024
    nt = n_ctx + seq
    tiles_per_b = nt // TM
    t_all = n_batch * nt
    n_tiles = t_all // TM
    n_assign = t_all * MOE_TOP_K
    n_blocks = -(-(n_assign + MOE_EXPERTS * (RUN_ALIGN - 1)) // EXPERT_BLOCK) + MOE_EXPERTS
    n_rows = n_blocks * EXPERT_BLOCK

    xs = jnp.concatenate([ctx, x], axis=1).reshape(t_all, d)
    cond8 = jnp.concatenate([c, c_ctx[None], jnp.zeros((8 - n_batch - 1, d), f32)], axis=0)
    mods = _mod_call(cond8, w_mod, b_mod)
    mods = mods.reshape(depth, 8 * N_MOD, 1, d)

    mla_tabs = _rope_tables(seq, n_ctx, MLA_ROPE // 4)
    gqa_tabs = _rope_tables(seq, n_ctx, GQA_HEAD_DIM // 4)
    ltri = jnp.tril(jnp.ones((TM, TM), f32), -1).astype(bf16)
    upper = jnp.triu(jnp.ones((LANES, LANES), f32), 1)
    mspec = functools.partial(_mod_spec, tiles_per_b=tiles_per_b, n_batch=n_batch)
    tab_specs = [_table_spec(tiles_per_b)] * 3
    route_in, route_out = _route_specs(d, tiles_per_b, n_batch)
    route_shapes = _route_out_shapes(t_all, d)
    route_scratch = [pltpu.VMEM((1, LANES), f32)]

    for i in range(depth):
        kind, j = i % N_MIXERS, i // N_MIXERS
        m3 = mods[i]
        gmix = norm_mix_g[i][None]
        head_in = [_tile_spec(d), mspec(0), mspec(1), _row_spec(d)]
        wr = jnp.pad(jnp.concatenate([moe_w_group[i], moe_w_expert[i]], axis=1),
                     ((0, 0), (0, LANES - MOE_GROUPS - MOE_EXPERTS)))
        wr_hi = wr.astype(bf16)
        wr = jnp.stack([wr_hi, (wr - wr_hi.astype(f32)).astype(bf16)])
        br = jnp.pad(jnp.concatenate([moe_b_group[i], moe_b_expert[i]]),
                     (0, LANES - MOE_GROUPS - MOE_EXPERTS))[None]
        route_args = (m3, norm_ffn_g[i][None], m3, m3, wr, br, ltri, upper)

        if kind == 0:
            dk, dv = 2 * LANES, MLA_V
            wcat = jnp.pad(jnp.concatenate([mla_w_dq[j], mla_w_dkv[j]], axis=1),
                           ((0, 0), (0, LANES - MLA_ROPE))).astype(bf16)
            wuq = jnp.pad(mla_w_uq[j].reshape(MLA_Q_LORA, MLA_HEADS, MLA_NOPE + MLA_ROPE),
                          ((0, 0), (0, 0), (0, dk - MLA_NOPE - MLA_ROPE))
                          ).reshape(MLA_Q_LORA, MLA_HEADS * dk).astype(bf16)
            wukv3 = mla_w_ukv[j].reshape(MLA_KV_LORA, MLA_HEADS, MLA_NOPE + MLA_V)
            wukv = jnp.concatenate([wukv3[..., :MLA_NOPE].reshape(MLA_KV_LORA, -1),
                                    wukv3[..., MLA_NOPE:].reshape(MLA_KV_LORA, -1)], axis=1).astype(bf16)
            q, k, v = pl.pallas_call(
                functools.partial(_mla_proj_body, scale=(MLA_NOPE + MLA_ROPE) ** -0.5 * LOG2E),
                grid=(n_tiles,),
                in_specs=head_in + [_full_spec(wcat.shape), _row_spec(MLA_Q_LORA), _row_spec(MLA_KV_LORA),
                                    _full_spec(wuq.shape), _full_spec(wukv.shape)] + tab_specs,
                out_specs=[_tile_spec(MLA_HEADS * dk), _tile_spec(MLA_HEADS * dk), _tile_spec(MLA_HEADS * dv)],
                out_shape=[jax.ShapeDtypeStruct((t_all, MLA_HEADS * dk), bf16),
                           jax.ShapeDtypeStruct((t_all, MLA_HEADS * dk), bf16),
                           jax.ShapeDtypeStruct((t_all, MLA_HEADS * dv), bf16)],
                compiler_params=_params("arbitrary"),
                name="mla_proj",
            )(xs, m3, m3, gmix, wcat, mla_g_q[j][None], mla_g_kv[j][None], wuq, wukv, *mla_tabs)
            att = _attn_call(q, k, v, n_batch, MLA_HEADS, MLA_HEADS, dk, dv, n_ctx)
            w_o = mla_w_o[j].astype(bf16)
        elif kind == 2:
            dk = dv = GQA_HEAD_DIM
            wqkv = gqa_w_qkv[j].astype(bf16)
            q, k, v = pl.pallas_call(
                functools.partial(_gqa_proj_body, scale=GQA_HEAD_DIM ** -0.5 * LOG2E),
                grid=(n_tiles,),
                in_specs=head_in + [_full_spec(wqkv.shape), _row_spec(dk), _row_spec(dk)] + tab_specs,
                out_specs=[_tile_spec(GQA_HEADS * dk), _tile_spec(GQA_KV_HEADS * dk),
                           _tile_spec(GQA_KV_HEADS * dv)],
                out_shape=[jax.ShapeDtypeStruct((t_all, GQA_HEADS * dk), bf16),
                           jax.ShapeDtypeStruct((t_all, GQA_KV_HEADS * dk), bf16),
                           jax.ShapeDtypeStruct((t_all, GQA_KV_HEADS * dv), bf16)],
                compiler_params=_params("arbitrary"),
                name="gqa_proj",
            )(xs, m3, m3, gmix, wqkv, gqa_q_norm_g[j][None], gqa_k_norm_g[j][None], *gqa_tabs)
            att = _attn_call(q, k, v, n_batch, GQA_HEADS, GQA_KV_HEADS, dk, dv, n_ctx)
            w_o = gqa_w_o[j].astype(bf16)

        if kind == 1:
            w_in = conv_w_in[j].astype(bf16)
            bg, cu = pl.pallas_call(
                _conv_proj_body,
                grid=(n_tiles,),
                in_specs=head_in + [_full_spec(w_in.shape)],
                out_specs=[_tile_spec(d), _tile_spec(d)],
                out_shape=[jax.ShapeDtypeStruct((t_all, d), bf16)] * 2,
                compiler_params=_params("arbitrary"),
                name="conv_proj",
            )(xs, m3, m3, gmix, w_in)
            w_o = conv_w_out[j].astype(bf16)
            per = TM // HALO
            last = t_all // HALO - 1
            xs, f, meta, wts, runs, cnt = pl.pallas_call(
                functools.partial(_conv_out_body, tiles_per_b=tiles_per_b),
                grid=(n_tiles,),
                in_specs=[_tile_spec(d),
                          pl.BlockSpec((HALO, d), lambda i: (jnp.maximum(i * per - 1, 0), 0)),
                          pl.BlockSpec((HALO, d), lambda i: (jnp.minimum((i + 1) * per, last), 0)),
                          _tile_spec(d), _full_spec((3, d)), _full_spec(w_o.shape), _tile_spec(d)] + route_in,
                out_specs=route_out,
                out_shape=route_shapes,
                scratch_shapes=route_scratch,
                compiler_params=_params("arbitrary"),
                name="conv_out_route",
            )(cu, cu, cu, bg, conv_w[j], w_o, xs, *route_args)
        else:
            xs, f, meta, wts, runs, cnt = pl.pallas_call(
                _attn_out_body,
                grid=(n_tiles,),
                in_specs=[_tile_spec(att.shape[-1]), _full_spec(w_o.shape), _tile_spec(d)] + route_in,
                out_specs=route_out,
                out_shape=route_shapes,
                scratch_shapes=route_scratch,
                compiler_params=_params("arbitrary"),
                name="attn_out_route",
            )(att, w_o, xs, *route_args)

        counts = cnt[0, ROUTE_LANE0:ROUTE_LANE0 + MOE_EXPERTS].astype(jnp.int32)
        padded = (counts + RUN_ALIGN - 1 + EXPERT_BLOCK - 1) // EXPERT_BLOCK * EXPERT_BLOCK
        pends = jnp.cumsum(padded)
        pstarts = pends - padded
        blk_row0 = jnp.arange(n_blocks, dtype=jnp.int32) * EXPERT_BLOCK
        blk_e = jnp.minimum(jnp.sum((pends[None, :] <= blk_row0[:, None]).astype(jnp.int32), axis=1),
                            MOE_EXPERTS - 1)
        n_act = (pends[-1:] // EXPERT_BLOCK).astype(jnp.int32)
        seg = jnp.concatenate([pstarts, pstarts + counts, padded - counts, n_act,
                               jnp.zeros((MOE_EXPERTS - 1,), jnp.int32)]).astype(jnp.int32)

        xp = _dispatch_call(seg, runs, meta, f, n_rows)
        yp = _expert_call(blk_e, n_act, xp, moe_w_gate, moe_w_up, moe_w_down, i)
        final = i == depth - 1
        xs = _combine_call(seg, runs, xs, wts, m3, final_norm_g[None], yp, n_batch, tiles_per_b, final)

    return xs.reshape(n_batch, seq, d)
```

```python
import functools

import jax
import jax.numpy as jnp
from jax import lax
from jax.experimental import pallas as pl
from jax.experimental.pallas import tpu as pltpu

f32 = jnp.float32
bf16 = jnp.bfloat16

GRID_W = 64
ROPE_THETA = 10000.0
NORM_EPS = 1e-6
N_MOD = 6
N_MIXERS = 3
MLA_HEADS, MLA_Q_LORA, MLA_KV_LORA, MLA_NOPE, MLA_ROPE, MLA_V = 8, 512, 256, 128, 64, 128
GQA_HEADS, GQA_KV_HEADS, GQA_HEAD_DIM = 16, 8, 128
MOE_GROUPS, MOE_EXPERTS_PER_GROUP, MOE_TOP_K = 4, 8, 2
MOE_EXPERTS = MOE_GROUPS * MOE_EXPERTS_PER_GROUP

LANES = 128
TM = 256
EXPERT_BLOCK = 256
SUBLANES = 8
RUN_ALIGN = 8
PAD_PIECES = tuple(1 << b for b in reversed(range(EXPERT_BLOCK.bit_length())))
HALO = 16
ATTN_KEY_CHUNK = 256
ATTN_HEADS_PER_STEP = 4
LOG2E = 1.4426950408889634
VMEM_LIMIT = 56 * 1024 * 1024
ROUTE_LANE0 = MOE_GROUPS
META_ROWS = 8
MXU_TILE = 256
LOCAL_ROWS = -(-(TM * MOE_TOP_K + MOE_EXPERTS * (RUN_ALIGN - 1)) // MXU_TILE) * MXU_TILE
NEG = -1e30


def _params(*sem):
    return pltpu.CompilerParams(dimension_semantics=sem, vmem_limit_bytes=VMEM_LIMIT)


def _rms(x):
    return x * lax.rsqrt(jnp.mean(x * x, axis=-1, keepdims=True) + NORM_EPS)


def _modulate(x, gn, sh, sc):
    return _rms(x) * gn * (1.0 + sc) + sh


def _silu(x):
    return x / (1.0 + jnp.exp(-x))


def _rope(x, c, s):
    return x * c + pltpu.roll(x, LANES // 2, 1) * s


def _mod_body(cond_ref, w_ref, b_ref, o_ref):
    a = _silu(cond_ref[...])
    o_ref[...] = jnp.dot(a, w_ref[...], preferred_element_type=f32,
                         precision=lax.Precision.HIGHEST) + b_ref[...]


def _mod_call(cond8, w_mod, b_mod):
    depth, d, n6 = w_mod.shape
    tn = n6 // 4
    return pl.pallas_call(
        _mod_body,
        grid=(depth, n6 // tn),
        in_specs=[pl.BlockSpec((8, d), lambda l, j: (0, 0)),
                  pl.BlockSpec((None, d, tn), lambda l, j: (l, 0, j)),
                  pl.BlockSpec((None, 1, tn), lambda l, j: (l, 0, j))],
        out_specs=pl.BlockSpec((None, 8, tn), lambda l, j: (l, 0, j)),
        out_shape=jax.ShapeDtypeStruct((depth, 8, n6), f32),
        compiler_params=_params("arbitrary", "arbitrary"),
        name="mod_vectors",
    )(cond8, w_mod, b_mod.reshape(depth, 1, n6))


def _mod_spec(chunk, tiles_per_b, n_batch):
    def imap(i):
        i = jnp.minimum(i, tiles_per_b * n_batch - 1)
        r = jnp.where(i % tiles_per_b == 0, n_batch, i // tiles_per_b)
        return (r * N_MOD + chunk, 0, 0)
    return pl.BlockSpec((None, 1, 1024), imap)


def _row_spec(n):
    return pl.BlockSpec((1, n), lambda i: (0, 0))


def _full_spec(shape):
    return pl.BlockSpec(shape, lambda i: (0,) * len(shape))


def _tile_spec(n):
    return pl.BlockSpec((TM, n), lambda i: (i, 0))


def _mla_proj_body(x_ref, sh_ref, sc_ref, gn_ref, wcat_ref, gq_ref, gkv_ref, wuq_ref, wukv_ref,
                   c_ref, s_ref, q_ref, k_ref, v_ref, *, scale):
    a = _modulate(x_ref[...], gn_ref[...], sh_ref[...], sc_ref[...]).astype(bf16)
    z = jnp.dot(a, wcat_ref[...], preferred_element_type=f32)
    cq = (_rms(z[:, :MLA_Q_LORA]) * gq_ref[...]).astype(bf16)
    ckv = (_rms(z[:, MLA_Q_LORA:MLA_Q_LORA + MLA_KV_LORA]) * gkv_ref[...]).astype(bf16)
    kpe = z[:, MLA_Q_LORA + MLA_KV_LORA:]
    q = jnp.dot(cq, wuq_ref[...], preferred_element_type=f32)
    kv = jnp.dot(ckv, wukv_ref[...], preferred_element_type=f32)
    c, s = c_ref[...], s_ref[...]
    kpe_r = _rope(kpe, c, s).astype(bf16)
    hd = 2 * LANES
    for h in range(MLA_HEADS):
        q_ref[:, h * hd:h * hd + LANES] = (q[:, h * hd:h * hd + LANES] * scale).astype(bf16)
        q_ref[:, h * hd + LANES:(h + 1) * hd] = (
            _rope(q[:, h * hd + LANES:(h + 1) * hd], c, s) * scale).astype(bf16)
        k_ref[:, h * hd:h * hd + LANES] = kv[:, h * LANES:(h + 1) * LANES].astype(bf16)
        k_ref[:, h * hd + LANES:(h + 1) * hd] = kpe_r
    v_ref[...] = kv[:, MLA_HEADS * MLA_NOPE:].astype(bf16)


def _gqa_proj_body(x_ref, sh_ref, sc_ref, gn_ref, w_ref, qg_ref, kg_ref,
                   c_ref, s_ref, q_ref, k_ref, v_ref, *, scale):
    a = _modulate(x_ref[...], gn_ref[...], sh_ref[...], sc_ref[...]).astype(bf16)
    z = jnp.dot(a, w_ref[...], preferred_element_type=f32)
    c, s = c_ref[...], s_ref[...]
    dq = GQA_HEADS * GQA_HEAD_DIM
    dkv = GQA_KV_HEADS * GQA_HEAD_DIM
    qg = qg_ref[...] * scale
    kg = kg_ref[...]
    for h in range(GQA_HEADS):
        qh = _rms(z[:, h * LANES:(h + 1) * LANES]) * qg
        q_ref[:, h * LANES:(h + 1) * LANES] = _rope(qh, c, s).astype(bf16)
    for h in range(GQA_KV_HEADS):
        kh = _rms(z[:, dq + h * LANES:dq + (h + 1) * LANES]) * kg
        k_ref[:, h * LANES:(h + 1) * LANES] = _rope(kh, c, s).astype(bf16)
    v_ref[...] = z[:, dq + dkv:].astype(bf16)


def _conv_proj_body(x_ref, sh_ref, sc_ref, gn_ref, w_ref, bg_ref, cu_ref):
    a = _modulate(x_ref[...], gn_ref[...], sh_ref[...], sc_ref[...]).astype(bf16)
    z = jnp.dot(a, w_ref[...], preferred_element_type=f32)
    d = bg_ref.shape[-1]
    bg_ref[...] = z[:, :d].astype(bf16)
    cu_ref[...] = (z[:, d:2 * d] * z[:, 2 * d:]).astype(bf16)


def _lane_groups(x, op):
    parts = [x[:, i * LANES:(i + 1) * LANES] for i in range(x.shape[1] // LANES)]
    while len(parts) > 1:
        parts = [op(parts[i], parts[i + 1]) if i + 1 < len(parts) else parts[i]
                 for i in range(0, len(parts), 2)]
    return parts[0]


def _attn_body(q_ref, k_ref, v_ref, o_ref, s_scr, *, n_ctx, n_all, heads_per_step, group, dk, dv):
    def run(nk):
        chunks = [(a, min(a + ATTN_KEY_CHUNK, nk)) for a in range(0, nk, ATTN_KEY_CHUNK)]
        for j in range(heads_per_step):
            kv = j // group
            q = q_ref[:, j * dk:(j + 1) * dk]
            m = None
            for a, b in chunks:
                s = lax.dot_general(q, k_ref[a:b, kv * dk:(kv + 1) * dk], (((1,), (1,)), ((), ())),
                                    preferred_element_type=f32)
                s_scr[j, :, a:b] = s
                mc = _lane_groups(s, jnp.maximum)
                m = mc if m is None else jnp.maximum(m, mc)
            m = jnp.max(m, axis=-1, keepdims=True)
            o = l = None
            for a, b in chunks:
                p = jnp.exp2(s_scr[j, :, a:b] - m)
                lc = _lane_groups(p, jnp.add)
                oc = jnp.dot(p.astype(bf16), v_ref[a:b, kv * dv:(kv + 1) * dv], preferred_element_type=f32)
                o, l = (oc, lc) if o is None else (o + oc, l + lc)
            l = jnp.sum(l, axis=-1, keepdims=True)
            o_ref[:, j * dv:(j + 1) * dv] = (o / l).astype(o_ref.dtype)

    i = pl.program_id(2)
    pl.when(i == 0)(lambda: run(n_ctx))
    pl.when(i > 0)(lambda: run(n_all))


def _attn_call(q, k, v, n_batch, heads, kv_heads, dk, dv, n_ctx):
    nt = q.shape[0] // n_batch
    g = heads // kv_heads
    hp = ATTN_HEADS_PER_STEP
    kh = hp // g
    q3, k3, v3 = (a.reshape(n_batch, nt, a.shape[-1]) for a in (q, k, v))
    out = pl.pallas_call(
        functools.partial(_attn_body, n_ctx=n_ctx, n_all=nt, heads_per_step=hp, group=g, dk=dk, dv=dv),
        grid=(n_batch, heads // hp, nt // TM),
        in_specs=[pl.BlockSpec((None, TM, hp * dk), lambda b, h, i: (b, i, h)),
                  pl.BlockSpec((None, nt, kh * dk), lambda b, h, i: (b, 0, h)),
                  pl.BlockSpec((None, nt, kh * dv), lambda b, h, i: (b, 0, h))],
        out_specs=pl.BlockSpec((None, TM, hp * dv), lambda b, h, i: (b, i, h)),
        out_shape=jax.ShapeDtypeStruct((n_batch, nt, heads * dv), bf16),
        scratch_shapes=[pltpu.VMEM((hp, TM, nt), f32)],
        compiler_params=_params("arbitrary", "arbitrary", "arbitrary"),
        name="attention",
    )(q3, k3, v3)
    return out.reshape(n_batch * nt, heads * dv)


def _route_epilogue(x, y, g1_ref, gn2_ref, sh2_ref, sc2_ref, wr_ref, br_ref, ltri_ref, upper_ref,
                    xo_ref, f_ref, meta_ref, wts_ref, runs_ref, cnt_ref, carry_ref, lg_scr):
    i = pl.program_id(0)

    @pl.when(i == 0)
    def _():
        carry_ref[...] = jnp.zeros_like(carry_ref)
        lg_scr[...] = jnp.zeros_like(lg_scr)

    xn = x + g1_ref[...] * y
    xo_ref[...] = xn
    f = _modulate(xn, gn2_ref[...], sh2_ref[...], sc2_ref[...])
    f_hi = f.astype(bf16)
    f_ref[...] = f_hi
    f_lo = (f - f_hi.astype(f32)).astype(bf16)
    logits = lg_scr[(i + 1) % 2]
    lg_scr[i % 2] = (jnp.dot(f_hi, wr_ref[0], preferred_element_type=f32)
                     + jnp.dot(f_lo, wr_ref[0], preferred_element_type=f32)
                     + jnp.dot(f_hi, wr_ref[1], preferred_element_type=f32))
    routed = (i > 0).astype(f32)
    bias = br_ref[...]

    def pick(onehot, v):
        return jnp.sum(jnp.where(onehot, v, 0.0), axis=-1, keepdims=True)

    def route_rows(lg):
        lane = lax.broadcasted_iota(jnp.int32, lg.shape, 1).astype(f32)

        def first_argmax(score):
            top = jnp.max(score, axis=-1, keepdims=True)
            return jnp.min(jnp.where(score == top, lane, float(LANES)), axis=-1, keepdims=True)

        def masked_softmax(mask):
            z = jnp.where(mask, lg, NEG)
            e = jnp.exp(z - jnp.max(z, axis=-1, keepdims=True))
            return e / jnp.sum(e, axis=-1, keepdims=True)

        gmask = lane < float(MOE_GROUPS)
        gp = masked_softmax(gmask)
        g_sel = first_argmax(jnp.where(gmask, gp + bias, NEG))
        g_w = pick(lane == g_sel, gp)
        lo = float(ROUTE_LANE0) + float(MOE_EXPERTS_PER_GROUP) * g_sel
        emask = (lane >= lo) & (lane < lo + float(MOE_EXPERTS_PER_GROUP))
        ep = masked_softmax(emask)
        escore = jnp.where(emask, ep + bias, NEG)
        i1 = first_argmax(escore)
        oh1 = lane == i1
        i2 = first_argmax(jnp.where(oh1, NEG, escore))
        oh2 = lane == i2
        p1, p2 = pick(oh1, ep), pick(oh2, ep)
        return (jnp.where(oh1, 1.0, 0.0), jnp.where(oh2, 1.0, 0.0),
                p1 / (p1 + p2) * g_w, p2 / (p1 + p2) * g_w)

    m1, m2, w1, w2 = route_rows(logits)
    oh1, oh2 = m1 > 0.5, m2 > 0.5
    lane = lax.broadcasted_iota(jnp.int32, logits.shape, 1).astype(f32)

    hits = (m1 + m2) * routed
    before = jnp.dot(ltri_ref[...], hits.astype(bf16), preferred_element_type=f32)
    n_tile = jnp.sum(hits, axis=0, keepdims=True)
    n_al = jnp.floor((n_tile + float(RUN_ALIGN - 1)) * (1.0 / RUN_ALIGN)) * float(RUN_ALIGN)
    lo = jnp.dot(jnp.broadcast_to(n_al, (SUBLANES, LANES)), upper_ref[...], preferred_element_type=f32,
                 precision=lax.Precision.HIGHEST)[0:1, :]
    lp1, lp2 = pick(oh1, before + lo), pick(oh2, before + lo)

    row = lax.broadcasted_iota(jnp.int32, runs_ref.shape, 0)
    runs = jnp.where(row == 0, n_tile, jnp.where(row == 1, carry_ref[...], jnp.where(row == 2, lo, 0.0)))
    runs_ref[...] = runs.astype(jnp.int32)
    carry_ref[...] = carry_ref[...] + n_tile
    cnt_ref[...] = carry_ref[...]

    meta = jnp.where(lane == 0.0, lp1, jnp.where(lane == 1.0, lp2, 0.0))
    meta_ref[...] = meta.T[:meta_ref.shape[0], :].astype(jnp.int32)
    col = lax.broadcasted_iota(jnp.int32, wts_ref.shape, 1)
    wts_ref[...] = jnp.where(col == 0, w1, jnp.where(col == 1, w2, jnp.where(col == 2, lp1, lp2)))


def _attn_out_body(y_ref, wo_ref, x_ref, *rest):
    y = jnp.dot(y_ref[...], wo_ref[...], preferred_element_type=f32)
    _route_epilogue(x_ref[...], y, *rest)


def _conv_out_body(cu_ref, prev_ref, next_ref, bg_ref, cw_ref, wo_ref, x_ref, *rest, tiles_per_b):
    t = jnp.minimum(pl.program_id(0), pl.num_programs(0) - 2) % tiles_per_b
    has_prev = (t >= 2).astype(f32)
    has_next = ((t != 0) & (t != tiles_per_b - 1)).astype(f32)
    cu = cu_ref[...].astype(f32)
    row = lax.broadcasted_iota(jnp.int32, cu.shape, 0)
    hp = prev_ref[HALO - 1:HALO, :].astype(f32) * has_prev
    hn = next_ref[0:1, :].astype(f32) * has_next
    up = jnp.where(row == 0, hp, pltpu.roll(cu, 1, 0))
    dn = jnp.where(row == TM - 1, hn, pltpu.roll(cu, TM - 1, 0))
    cw = cw_ref[...]
    z = cw[0:1, :] * up + cw[1:2, :] * cu + cw[2:3, :] * dn
    yb = (bg_ref[...].astype(f32) * z).astype(bf16)
    y = jnp.dot(yb, wo_ref[...], preferred_element_type=f32)
    _route_epilogue(x_ref[...], y, *rest)


def _mixer_tile_spec(n, last):
    return pl.BlockSpec((TM, n), lambda i: (jnp.minimum(i, last), 0))


def _route_specs(d, tiles_per_b, n_batch):
    last = tiles_per_b * n_batch - 1

    def routed(i):
        return jnp.maximum(i - 1, 0)

    ins = [_mod_spec(2, tiles_per_b, n_batch), _row_spec(d), _mod_spec(3, tiles_per_b, n_batch),
           _mod_spec(4, tiles_per_b, n_batch), _full_spec((2, d, LANES)), _row_spec(LANES),
           _full_spec((TM, TM)), _full_spec((LANES, LANES))]
    outs = [_mixer_tile_spec(d, last), _mixer_tile_spec(d, last),
            pl.BlockSpec((None, META_ROWS, TM), lambda i: (routed(i), 0, 0)),
            pl.BlockSpec((TM, 8), lambda i: (routed(i), 0)),
            pl.BlockSpec((None, SUBLANES, LANES), lambda i: (routed(i), 0, 0)), _row_spec(LANES)]
    return ins, outs


def _route_out_shapes(t, d):
    return [jax.ShapeDtypeStruct((t, d), f32), jax.ShapeDtypeStruct((t, d), bf16),
            jax.ShapeDtypeStruct((t // TM, META_ROWS, TM), jnp.int32), jax.ShapeDtypeStruct((t, 8), f32),
            jax.ShapeDtypeStruct((t // TM, SUBLANES, LANES), jnp.int32),
            jax.ShapeDtypeStruct((1, LANES), f32)]


def _rows_to_tiles(tref, x):
    n = x.shape[0]
    for s in range(SUBLANES):
        tref[pl.ds(s, n, stride=SUBLANES), :] = x[:, s * LANES:(s + 1) * LANES]


def _tiles_to_rows(tref):
    n = tref.shape[0] // SUBLANES
    return jnp.concatenate([tref[pl.ds(s, n, stride=SUBLANES), :] for s in range(SUBLANES)], axis=-1)


def _tile_rows(ref, row, n):
    return ref.at[pl.ds(row * SUBLANES, n * SUBLANES)]


def _run_copies(seg_ref, runs_ref, copy):
    def per_expert(e, total):
        lane = ROUTE_LANE0 + e
        n_chunks = (runs_ref[0, lane] + RUN_ALIGN - 1) // RUN_ALIGN
        slot0 = seg_ref[e] + runs_ref[1, lane]
        local0 = runs_ref[2, lane]

        def chunk(c, carry):
            copy(local0 + c * RUN_ALIGN, slot0 + c * RUN_ALIGN).start()
            return carry

        lax.fori_loop(0, n_chunks, chunk, 0)
        return total + n_chunks

    return lax.fori_loop(0, MOE_EXPERTS, per_expert, 0)


def _wait_chunks(n_chunks, copy):
    def wait_one(c, carry):
        copy(0, 0).wait()
        return carry
    lax.fori_loop(0, n_chunks, wait_one, 0)


def _dispatch_body(seg_ref, runs_ref, meta_ref, f_ref, xp_ref, sbuf, zbuf, pending, sem, zsem):
    i = pl.program_id(0)
    n_blocks = xp_ref.shape[0] // (EXPERT_BLOCK * SUBLANES)

    @pl.when(i == 0)
    def _():
        zbuf[...] = jnp.zeros_like(zbuf)

        def pad_fill(e, wait):
            n_pad = seg_ref[2 * MOE_EXPERTS + e]
            off = seg_ref[MOE_EXPERTS + e]
            for piece in PAD_PIECES:
                take = (n_pad & piece) != 0
                cp = pltpu.make_async_copy(_tile_rows(zbuf, 0, piece), _tile_rows(xp_ref, off, piece), zsem)
                pl.when(take)(cp.wait if wait else cp.start)
                off = off + jnp.where(take, piece, 0)

        def tail_fill(b, wait):
            cp = pltpu.make_async_copy(zbuf, _tile_rows(xp_ref, b * EXPERT_BLOCK, EXPERT_BLOCK), zsem)
            cp.wait() if wait else cp.start()

        for wait in (False, True):
            lax.fori_loop(0, MOE_EXPERTS, lambda e, c, w=wait: (pad_fill(e, w), c)[1], 0)
            lax.fori_loop(seg_ref[3 * MOE_EXPERTS], n_blocks, lambda b, c, w=wait: (tail_fill(b, w), c)[1], 0)

        pending[0] = 0

    buf = sbuf.at[i % 2]
    pos = lax.broadcasted_iota(jnp.int32, (LOCAL_ROWS, TM), 0)
    m = meta_ref[...]
    perm = jnp.where((pos == m[0:1, :]) | (pos == m[1:2, :]), 1.0, 0.0).astype(bf16)
    _rows_to_tiles(buf, jnp.dot(perm, f_ref[...], preferred_element_type=f32))

    def copy(local_row, slot_row):
        return pltpu.make_async_copy(_tile_rows(buf, local_row, RUN_ALIGN),
                                     _tile_rows(xp_ref, slot_row, RUN_ALIGN), sem)

    _wait_chunks(pending[0], copy)
    pending[0] = _run_copies(seg_ref, runs_ref, copy)

    @pl.when(i == pl.num_programs(0) - 1)
    def _():
        _wait_chunks(pending[0], copy)


def _dispatch_call(seg, runs, meta, f, n_rows):
    t, d = f.shape
    grid_spec = pltpu.PrefetchScalarGridSpec(
        num_scalar_prefetch=1,
        grid=(t // TM,),
        in_specs=[pl.BlockSpec((None, SUBLANES, LANES), lambda i, seg: (i, 0, 0), memory_space=pltpu.SMEM),
                  pl.BlockSpec((None, META_ROWS, TM), lambda i, seg: (i, 0, 0)),
                  pl.BlockSpec((TM, d), lambda i, seg: (i, 0))],
        out_specs=pl.BlockSpec(memory_space=pl.ANY),
        scratch_shapes=[pltpu.VMEM((2, LOCAL_ROWS * SUBLANES, LANES), f32),
                        pltpu.VMEM((EXPERT_BLOCK * SUBLANES, LANES), f32),
                        pltpu.SMEM((1,), jnp.int32),
                        pltpu.SemaphoreType.DMA(()), pltpu.SemaphoreType.DMA(())],
    )
    return pl.pallas_call(
        _dispatch_body,
        grid_spec=grid_spec,
        out_shape=jax.ShapeDtypeStruct((n_rows * SUBLANES, LANES), f32),
        compiler_params=_params("arbitrary"),
        name="moe_dispatch",
    )(seg, runs, meta, f)


def _expert_body(be_ref, na_ref, x_ref, wg_hbm, wu_hbm, wd_hbm, o_ref,
                 wg_f, wu_f, wd_f, wg_s, wu_s, wd_s, wsem, *, layer):
    i = pl.program_id(0)
    active = i < na_ref[0]
    e = be_ref[i]
    changed = (i == 0) | (e != be_ref[jnp.maximum(i - 1, 0)])

    def fetch(expert):
        pairs = ((wg_hbm, wg_f), (wu_hbm, wu_f), (wd_hbm, wd_f))
        return [pltpu.make_async_copy(w.at[layer, expert], buf, wsem.at[j]) for j, (w, buf) in enumerate(pairs)]

    @pl.when(i == 0)
    def _():
        for cp in fetch(e):
            cp.start()

    @pl.when(active & changed)
    def _():
        for cp in fetch(e):
            cp.wait()
        wg_s[...] = wg_f[...].astype(bf16)
        wu_s[...] = wu_f[...].astype(bf16)
        wd_s[...] = wd_f[...].astype(bf16)

        @pl.when(e + 1 < MOE_EXPERTS)
        def _():
            for cp in fetch(e + 1):
                cp.start()

    @pl.when(active)
    def _():
        x = _tiles_to_rows(x_ref).astype(bf16)
        g = jnp.dot(x, wg_s[...], preferred_element_type=f32)
        u = jnp.dot(x, wu_s[...], preferred_element_type=f32)
        h = (_silu(g) * u).astype(bf16)
        _rows_to_tiles(o_ref, jnp.dot(h, wd_s[...], preferred_element_type=f32))

    @pl.when(jnp.logical_not(active))
    def _():
        o_ref[...] = jnp.zeros_like(o_ref)


def _expert_call(blk_e, n_act, xp, w_gate, w_up, w_down, layer):
    n_rows = xp.shape[0] // SUBLANES
    d, ff = w_gate.shape[-2:]
    blk_rows = EXPERT_BLOCK * SUBLANES

    def blk(i, be, na):
        return jnp.maximum(jnp.minimum(i, na[0] - 1), 0)

    any_spec = pl.BlockSpec(memory_space=pl.ANY)
    grid_spec = pltpu.PrefetchScalarGridSpec(
        num_scalar_prefetch=2,
        grid=(n_rows // EXPERT_BLOCK,),
        in_specs=[pl.BlockSpec((blk_rows, LANES), lambda i, be, na: (blk(i, be, na), 0)),
                  any_spec, any_spec, any_spec],
        out_specs=pl.BlockSpec((blk_rows, LANES), lambda i, be, na: (i, 0)),
        scratch_shapes=[pltpu.VMEM((d, ff), f32), pltpu.VMEM((d, ff), f32), pltpu.VMEM((ff, d), f32),
                        pltpu.VMEM((d, ff), bf16), pltpu.VMEM((d, ff), bf16), pltpu.VMEM((ff, d), bf16),
                        pltpu.SemaphoreType.DMA((3,))],
    )
    return pl.pallas_call(
        functools.partial(_expert_body, layer=layer),
        grid_spec=grid_spec,
        out_shape=jax.ShapeDtypeStruct((n_rows * SUBLANES, LANES), f32),
        compiler_params=_params("arbitrary"),
        name="moe_experts",
    )(blk_e, n_act, xp, w_gate, w_up, w_down)


def _combine_body(seg_ref, runs_ref, next_runs_ref, x_ref, wts_ref, g2_ref, fg_ref, yp_ref, o_ref,
                  gbuf, pending, sem, *, final):
    step = pl.program_id(0) * pl.num_programs(1) + pl.program_id(1)
    n_steps = pl.num_programs(0) * pl.num_programs(1)
    cur = step % 2

    def fetch(buf):
        def copy(local_row, slot_row):
            return pltpu.make_async_copy(_tile_rows(yp_ref, slot_row, RUN_ALIGN),
                                         _tile_rows(gbuf.at[buf], local_row, RUN_ALIGN), sem.at[buf])
        return copy

    @pl.when(step == 0)
    def _():
        gbuf[...] = jnp.zeros_like(gbuf)
        pending[0] = _run_copies(seg_ref, runs_ref, fetch(0))

    @pl.when(step + 1 < n_steps)
    def _():
        pending[1 - cur] = _run_copies(seg_ref, next_runs_ref, fetch(1 - cur))

    _wait_chunks(pending[cur], fetch(cur))
    ys = _tiles_to_rows(gbuf.at[cur]).astype(bf16)
    w = wts_ref[...]
    pos = lax.broadcasted_iota(jnp.int32, (TM, LOCAL_ROWS), 1).astype(f32)
    y = None
    for k in range(MOE_TOP_K):
        pick = jnp.where(pos == w[:, MOE_TOP_K + k:MOE_TOP_K + k + 1], 1.0, 0.0).astype(bf16)
        yk = w[:, k:k + 1] * jnp.dot(pick, ys, preferred_element_type=f32)
        y = yk if y is None else y + yk
    xn = x_ref[...] + g2_ref[...] * y
    if final:
        xn = _rms(xn) * fg_ref[...]
    o_ref[...] = xn


def _combine_call(seg, runs, x, wts, mods3, final_g, yp, n_batch, tiles_per_b, final):
    t, d = x.shape
    t0 = 1 if final else 0
    nt = tiles_per_b - t0

    def tile(b, j):
        return b * tiles_per_b + t0 + j

    def next_tile(b, j):
        nxt = jnp.minimum(b * nt + j + 1, n_batch * nt - 1)
        return tile(nxt // nt, nxt % nt)

    def g2_map(b, j, seg):
        r = jnp.where(t0 + j == 0, n_batch, b)
        return (r * N_MOD + 5, 0, 0)

    grid_spec = pltpu.PrefetchScalarGridSpec(
        num_scalar_prefetch=1,
        grid=(n_batch, nt),
        in_specs=[pl.BlockSpec((None, SUBLANES, LANES), lambda b, j, seg: (tile(b, j), 0, 0),
                               memory_space=pltpu.SMEM),
                  pl.BlockSpec((None, SUBLANES, LANES), lambda b, j, seg: (next_tile(b, j), 0, 0),
                               memory_space=pltpu.SMEM),
                  pl.BlockSpec((TM, d), lambda b, j, seg: (tile(b, j), 0)),
                  pl.BlockSpec((TM, 8), lambda b, j, seg: (tile(b, j), 0)),
                  pl.BlockSpec((None, 1, d), g2_map),
                  pl.BlockSpec((1, d), lambda b, j, seg: (0, 0)),
                  pl.BlockSpec(memory_space=pl.ANY)],
        out_specs=pl.BlockSpec((TM, d), lambda b, j, seg: (b * nt + j, 0)),
        scratch_shapes=[pltpu.VMEM((2, LOCAL_ROWS * SUBLANES, LANES), f32), pltpu.SMEM((2,), jnp.int32),
                        pltpu.SemaphoreType.DMA((2,))],
    )
    return pl.pallas_call(
        functools.partial(_combine_body, final=final),
        grid_spec=grid_spec,
        out_shape=jax.ShapeDtypeStruct((n_batch * nt * TM, d), f32),
        compiler_params=_params("arbitrary", "arbitrary"),
        name="moe_combine",
    )(seg, runs, runs, x, wts, mods3, final_g, yp)


def _rope_tables(seq, n_ctx, half):
    inv = ROPE_THETA ** (-jnp.arange(half, dtype=f32) / half)
    t = jnp.arange(seq, dtype=jnp.int32)
    ang_r = (t // GRID_W).astype(f32)[:, None] * inv[None, :]
    ang_c = (t % GRID_W).astype(f32)[:, None] * inv[None, :]
    gap = jnp.zeros((seq, LANES // 2 - 2 * half), f32)
    cos_h = jnp.concatenate([jnp.cos(ang_r), jnp.cos(ang_c), gap], axis=-1)
    sin_h = jnp.concatenate([jnp.sin(ang_r), jnp.sin(ang_c), gap], axis=-1)
    cos = jnp.concatenate([cos_h, cos_h], axis=-1)
    sin = jnp.concatenate([-sin_h, sin_h], axis=-1)
    ctx_c = jnp.broadcast_to((cos[:1] != 0.0).astype(f32), (n_ctx, LANES))
    ctx_s = jnp.zeros((n_ctx, LANES), f32)
    return jnp.concatenate([ctx_c, cos], axis=0), jnp.concatenate([ctx_s, sin], axis=0)


def _rope_lanes(w, half):
    i = jnp.arange(half)
    gap = jnp.zeros(w.shape[:-1] + (LANES // 2 - 2 * half,), w.dtype)
    return jnp.concatenate([w[..., jnp.concatenate([i, 2 * half + i])], gap,
                            w[..., jnp.concatenate([half + i, 3 * half + i])], gap], axis=-1)


def _table_spec(tiles_per_b):
    return pl.BlockSpec((TM, LANES), lambda i: (i % tiles_per_b, 0))


def kernel(x, c, ctx, c_ctx, w_mod, b_mod, norm_mix_g, norm_ffn_g, mla_w_dq, mla_g_q, mla_w_uq, mla_w_dkv, mla_g_kv, mla_w_ukv, mla_w_o, conv_w_in, conv_w, conv_w_out, gqa_w_qkv, gqa_q_norm_g, gqa_k_norm_g, gqa_w_o, moe_w_group, moe_b_group, moe_w_expert, moe_b_expert, moe_w_gate, moe_w_up, moe_w_down, final_norm_g):
    n_batch, seq, d = x.shape
    n_ctx = ctx.shape[1]
    depth = w_mod.shape[0]
    assert n_ctx == TM and seq % TM == 0 and d == 1024
    nt = n_ctx + seq
    tiles_per_b = nt // TM
    t_all = n_batch * nt
    n_tiles = t_all // TM
    n_assign = t_all * MOE_TOP_K
    n_blocks = -(-(n_assign + MOE_EXPERTS * (RUN_ALIGN - 1)) // EXPERT_BLOCK) + MOE_EXPERTS
    n_rows = n_blocks * EXPERT_BLOCK

    xs = jnp.concatenate([ctx, x], axis=1).reshape(t_all, d)
    cond8 = jnp.concatenate([c, c_ctx[None], jnp.zeros((8 - n_batch - 1, d), f32)], axis=0)
    mods = _mod_call(cond8, w_mod, b_mod)
    mods = mods.reshape(depth, 8 * N_MOD, 1, d)

    mla_tabs = _rope_tables(seq, n_ctx, MLA_ROPE // 4)
    gqa_tabs = _rope_tables(seq, n_ctx, GQA_HEAD_DIM // 4)
    ltri = jnp.tril(jnp.ones((TM, TM), f32), -1).astype(bf16)
    upper = jnp.triu(jnp.ones((LANES, LANES), f32), 1)
    mspec = functools.partial(_mod_spec, tiles_per_b=tiles_per_b, n_batch=n_batch)
    tab_specs = [_table_spec(tiles_per_b)] * 2
    route_in, route_out = _route_specs(d, tiles_per_b, n_batch)
    route_shapes = _route_out_shapes(t_all, d)
    route_scratch = [pltpu.VMEM((1, LANES), f32), pltpu.VMEM((2, TM, LANES), f32)]
    mtile = functools.partial(_mixer_tile_spec, last=n_tiles - 1)

    for i in range(depth):
        kind, j = i % N_MIXERS, i // N_MIXERS
        m3 = mods[i]
        gmix = norm_mix_g[i][None]
        head_in = [_tile_spec(d), mspec(0), mspec(1), _row_spec(d)]
        wr = jnp.pad(jnp.concatenate([moe_w_group[i], moe_w_expert[i]], axis=1),
                     ((0, 0), (0, LANES - MOE_GROUPS - MOE_EXPERTS)))
        wr_hi = wr.astype(bf16)
        wr = jnp.stack([wr_hi, (wr - wr_hi.astype(f32)).astype(bf16)])
        br = jnp.pad(jnp.concatenate([moe_b_group[i], moe_b_expert[i]]),
                     (0, LANES - MOE_GROUPS - MOE_EXPERTS))[None]
        route_args = (m3, norm_ffn_g[i][None], m3, m3, wr, br, ltri, upper)

        if kind == 0:
            dk, dv = 2 * LANES, MLA_V
            half = MLA_ROPE // 4
            wcat = jnp.concatenate([mla_w_dq[j], mla_w_dkv[j][:, :MLA_KV_LORA],
                                    _rope_lanes(mla_w_dkv[j][:, MLA_KV_LORA:], half)], axis=1).astype(bf16)
            wuq3 = mla_w_uq[j].reshape(MLA_Q_LORA, MLA_HEADS, MLA_NOPE + MLA_ROPE)
            wuq = jnp.concatenate([wuq3[..., :MLA_NOPE], _rope_lanes(wuq3[..., MLA_NOPE:], half)], axis=-1
                                  ).reshape(MLA_Q_LORA, MLA_HEADS * dk).astype(bf16)
            wukv3 = mla_w_ukv[j].reshape(MLA_KV_LORA, MLA_HEADS, MLA_NOPE + MLA_V)
            wukv = jnp.concatenate([wukv3[..., :MLA_NOPE].reshape(MLA_KV_LORA, -1),
                                    wukv3[..., MLA_NOPE:].reshape(MLA_KV_LORA, -1)], axis=1).astype(bf16)
            q, k, v = pl.pallas_call(
                functools.partial(_mla_proj_body, scale=(MLA_NOPE + MLA_ROPE) ** -0.5 * LOG2E),
                grid=(n_tiles,),
                in_specs=head_in + [_full_spec(wcat.shape), _row_spec(MLA_Q_LORA), _row_spec(MLA_KV_LORA),
                                    _full_spec(wuq.shape), _full_spec(wukv.shape)] + tab_specs,
                out_specs=[_tile_spec(MLA_HEADS * dk), _tile_spec(MLA_HEADS * dk), _tile_spec(MLA_HEADS * dv)],
                out_shape=[jax.ShapeDtypeStruct((t_all, MLA_HEADS * dk), bf16),
                           jax.ShapeDtypeStruct((t_all, MLA_HEADS * dk), bf16),
                           jax.ShapeDtypeStruct((t_all, MLA_HEADS * dv), bf16)],
                compiler_params=_params("arbitrary"),
                name="mla_proj",
            )(xs, m3, m3, gmix, wcat, mla_g_q[j][None], mla_g_kv[j][None], wuq, wukv, *mla_tabs)
            att = _attn_call(q, k, v, n_batch, MLA_HEADS, MLA_HEADS, dk, dv, n_ctx)
            w_o = mla_w_o[j].astype(bf16)
        elif kind == 2:
            dk = dv = GQA_HEAD_DIM
            half = GQA_HEAD_DIM // 4
            n_qk = GQA_HEADS + GQA_KV_HEADS
            wqk = _rope_lanes(gqa_w_qkv[j][:, :n_qk * dk].reshape(d, n_qk, dk), half).reshape(d, n_qk * dk)
            wqkv = jnp.concatenate([wqk, gqa_w_qkv[j][:, n_qk * dk:]], axis=1).astype(bf16)
            q, k, v = pl.pallas_call(
                functools.partial(_gqa_proj_body, scale=GQA_HEAD_DIM ** -0.5 * LOG2E),
                grid=(n_tiles,),
                in_specs=head_in + [_full_spec(wqkv.shape), _row_spec(dk), _row_spec(dk)] + tab_specs,
                out_specs=[_tile_spec(GQA_HEADS * dk), _tile_spec(GQA_KV_HEADS * dk),
                           _tile_spec(GQA_KV_HEADS * dv)],
                out_shape=[jax.ShapeDtypeStruct((t_all, GQA_HEADS * dk), bf16),
                           jax.ShapeDtypeStruct((t_all, GQA_KV_HEADS * dk), bf16),
                           jax.ShapeDtypeStruct((t_all, GQA_KV_HEADS * dv), bf16)],
                compiler_params=_params("arbitrary"),
                name="gqa_proj",
            )(xs, m3, m3, gmix, wqkv, _rope_lanes(gqa_q_norm_g[j][None], half),
              _rope_lanes(gqa_k_norm_g[j][None], half), *gqa_tabs)
            att = _attn_call(q, k, v, n_batch, GQA_HEADS, GQA_KV_HEADS, dk, dv, n_ctx)
            w_o = gqa_w_o[j].astype(bf16)

        if kind == 1:
            w_in = conv_w_in[j].astype(bf16)
            bg, cu = pl.pallas_call(
                _conv_proj_body,
                grid=(n_tiles,),
                in_specs=head_in + [_full_spec(w_in.shape)],
                out_specs=[_tile_spec(d), _tile_spec(d)],
                out_shape=[jax.ShapeDtypeStruct((t_all, d), bf16)] * 2,
                compiler_params=_params("arbitrary"),
                name="conv_proj",
            )(xs, m3, m3, gmix, w_in)
            w_o = conv_w_out[j].astype(bf16)
            per = TM // HALO
            last = t_all // HALO - 1
            xs, f, meta, wts, runs, cnt = pl.pallas_call(
                functools.partial(_conv_out_body, tiles_per_b=tiles_per_b),
                grid=(n_tiles + 1,),
                in_specs=[mtile(d),
                          pl.BlockSpec((HALO, d), lambda i: (
                              jnp.maximum(jnp.minimum(i, n_tiles - 1) * per - 1, 0), 0)),
                          pl.BlockSpec((HALO, d), lambda i: (jnp.minimum((i + 1) * per, last), 0)),
                          mtile(d), _full_spec((3, d)), _full_spec(w_o.shape), mtile(d)] + route_in,
                out_specs=route_out,
                out_shape=route_shapes,
                scratch_shapes=route_scratch,
                compiler_params=_params("arbitrary"),
                name="conv_out_route",
            )(cu, cu, cu, bg, conv_w[j], w_o, xs, *route_args)
        else:
            xs, f, meta, wts, runs, cnt = pl.pallas_call(
                _attn_out_body,
                grid=(n_tiles + 1,),
                in_specs=[mtile(att.shape[-1]), _full_spec(w_o.shape), mtile(d)] + route_in,
                out_specs=route_out,
                out_shape=route_shapes,
                scratch_shapes=route_scratch,
                compiler_params=_params("arbitrary"),
                name="attn_out_route",
            )(att, w_o, xs, *route_args)

        counts = cnt[0, ROUTE_LANE0:ROUTE_LANE0 + MOE_EXPERTS].astype(jnp.int32)
        padded = (counts + RUN_ALIGN - 1 + EXPERT_BLOCK - 1) // EXPERT_BLOCK * EXPERT_BLOCK
        pends = jnp.cumsum(padded)
        pstarts = pends - padded
        blk_row0 = jnp.arange(n_blocks, dtype=jnp.int32) * EXPERT_BLOCK
        blk_e = jnp.minimum(jnp.sum((pends[None, :] <= blk_row0[:, None]).astype(jnp.int32), axis=1),
                            MOE_EXPERTS - 1)
        n_act = (pends[-1:] // EXPERT_BLOCK).astype(jnp.int32)
        seg = jnp.concatenate([pstarts, pstarts + counts, padded - counts, n_act,
                               jnp.zeros((MOE_EXPERTS - 1,), jnp.int32)]).astype(jnp.int32)

        xp = _dispatch_call(seg, runs, meta, f, n_rows)
        yp = _expert_call(blk_e, n_act, xp, moe_w_gate, moe_w_up, moe_w_down, i)
        final = i == depth - 1
        xs = _combine_call(seg, runs, xs, wts, m3, final_norm_g[None], yp, n_batch, tiles_per_b, final)

    return xs.reshape(n_batch, seq, d)
```

```python
import functools

import jax
import jax.numpy as jnp
from jax import lax
from jax.experimental import pallas as pl
from jax.experimental.pallas import tpu as pltpu

f32 = jnp.float32
bf16 = jnp.bfloat16

GRID_W = 64
ROPE_THETA = 10000.0
NORM_EPS = 1e-6
N_MOD = 6
N_MIXERS = 3
MLA_HEADS, MLA_Q_LORA, MLA_KV_LORA, MLA_NOPE, MLA_ROPE, MLA_V = 8, 512, 256, 128, 64, 128
GQA_HEADS, GQA_KV_HEADS, GQA_HEAD_DIM = 16, 8, 128
MOE_GROUPS, MOE_EXPERTS_PER_GROUP, MOE_TOP_K = 4, 8, 2
MOE_EXPERTS = MOE_GROUPS * MOE_EXPERTS_PER_GROUP

LANES = 128
TM = 256
EXPERT_BLOCK = 256
SUBLANES = 8
RUN_ALIGN = 8
PAD_PIECES = tuple(1 << b for b in reversed(range(EXPERT_BLOCK.bit_length())))
HALO = 16
ATTN_KEY_CHUNK = 256
ATTN_HEADS_PER_STEP = 4
LOG2E = 1.4426950408889634
VMEM_LIMIT = 56 * 1024 * 1024
ROUTE_LANE0 = MOE_GROUPS
META_ROWS = 8
MXU_TILE = 256
LOCAL_ROWS = -(-(TM * MOE_TOP_K + MOE_EXPERTS * (RUN_ALIGN - 1)) // MXU_TILE) * MXU_TILE
NEG = -1e30


def _params(*sem):
    return pltpu.CompilerParams(dimension_semantics=sem, vmem_limit_bytes=VMEM_LIMIT)


def _rms(x):
    return x * lax.rsqrt(jnp.mean(x * x, axis=-1, keepdims=True) + NORM_EPS)


def _modulate(x, gn, sh, sc):
    return _rms(x) * gn * (1.0 + sc) + sh


def _silu(x):
    return x / (1.0 + jnp.exp(-x))


def _rope(x, c, s):
    return x * c + pltpu.roll(x, LANES // 2, 1) * s


def _mod_body(cond_ref, w_ref, b_ref, o_ref):
    a = _silu(cond_ref[...])
    o_ref[...] = jnp.dot(a, w_ref[...], preferred_element_type=f32,
                         precision=lax.Precision.HIGHEST) + b_ref[...]


def _mod_call(cond8, w_mod, b_mod):
    depth, d, n6 = w_mod.shape
    tn = n6 // 4
    return pl.pallas_call(
        _mod_body,
        grid=(depth, n6 // tn),
        in_specs=[pl.BlockSpec((8, d), lambda l, j: (0, 0)),
                  pl.BlockSpec((None, d, tn), lambda l, j: (l, 0, j)),
                  pl.BlockSpec((None, 1, tn), lambda l, j: (l, 0, j))],
        out_specs=pl.BlockSpec((None, 8, tn), lambda l, j: (l, 0, j)),
        out_shape=jax.ShapeDtypeStruct((depth, 8, n6), f32),
        compiler_params=_params("arbitrary", "arbitrary"),
        name="mod_vectors",
    )(cond8, w_mod, b_mod.reshape(depth, 1, n6))


def _mod_spec(chunk, tiles_per_b, n_batch, d=SUBLANES * LANES):
    def imap(i):
        i = jnp.minimum(i, tiles_per_b * n_batch - 1)
        r = jnp.where(i % tiles_per_b == 0, n_batch, i // tiles_per_b)
        return (r * N_MOD + chunk, 0, 0)
    return pl.BlockSpec((None, 1, d), imap)


def _row_spec(n):
    return pl.BlockSpec((1, n), lambda i: (0, 0))


def _full_spec(shape):
    return pl.BlockSpec(shape, lambda i: (0,) * len(shape))


def _tile_spec(n):
    return pl.BlockSpec((TM, n), lambda i: (i, 0))


def _mla_proj_body(x_ref, sh_ref, sc_ref, gn_ref, wcat_ref, gq_ref, gkv_ref, wuq_ref, wukv_ref,
                   c_ref, s_ref, q_ref, k_ref, v_ref, *, scale):
    a = _modulate(x_ref[...], gn_ref[...], sh_ref[...], sc_ref[...]).astype(bf16)
    z = jnp.dot(a, wcat_ref[...], preferred_element_type=f32)
    cq = (_rms(z[:, :MLA_Q_LORA]) * gq_ref[...]).astype(bf16)
    ckv = (_rms(z[:, MLA_Q_LORA:MLA_Q_LORA + MLA_KV_LORA]) * gkv_ref[...]).astype(bf16)
    kpe = z[:, MLA_Q_LORA + MLA_KV_LORA:]
    q = jnp.dot(cq, wuq_ref[...], preferred_element_type=f32)
    kv = jnp.dot(ckv, wukv_ref[...], preferred_element_type=f32)
    c, s = c_ref[...], s_ref[...]
    kpe_r = _rope(kpe, c, s).astype(bf16)
    hd = 2 * LANES
    for h in range(MLA_HEADS):
        q_ref[:, h * hd:h * hd + LANES] = (q[:, h * hd:h * hd + LANES] * scale).astype(bf16)
        q_ref[:, h * hd + LANES:(h + 1) * hd] = (
            _rope(q[:, h * hd + LANES:(h + 1) * hd], c, s) * scale).astype(bf16)
        k_ref[:, h * hd:h * hd + LANES] = kv[:, h * LANES:(h + 1) * LANES].astype(bf16)
        k_ref[:, h * hd + LANES:(h + 1) * hd] = kpe_r
    v_ref[...] = kv[:, MLA_HEADS * MLA_NOPE:].astype(bf16)


def _gqa_proj_body(x_ref, sh_ref, sc_ref, gn_ref, w_ref, qg_ref, kg_ref,
                   c_ref, s_ref, q_ref, k_ref, v_ref, *, scale):
    a = _modulate(x_ref[...], gn_ref[...], sh_ref[...], sc_ref[...]).astype(bf16)
    z = jnp.dot(a, w_ref[...], preferred_element_type=f32)
    c, s = c_ref[...], s_ref[...]
    dq = GQA_HEADS * GQA_HEAD_DIM
    dkv = GQA_KV_HEADS * GQA_HEAD_DIM
    qg = qg_ref[...] * scale
    kg = kg_ref[...]
    for h in range(GQA_HEADS):
        qh = _rms(z[:, h * LANES:(h + 1) * LANES]) * qg
        q_ref[:, h * LANES:(h + 1) * LANES] = _rope(qh, c, s).astype(bf16)
    for h in range(GQA_KV_HEADS):
        kh = _rms(z[:, dq + h * LANES:dq + (h + 1) * LANES]) * kg
        k_ref[:, h * LANES:(h + 1) * LANES] = _rope(kh, c, s).astype(bf16)
    v_ref[...] = z[:, dq + dkv:].astype(bf16)


def _conv_proj_body(x_ref, sh_ref, sc_ref, gn_ref, w_ref, bg_ref, cu_ref):
    a = _modulate(x_ref[...], gn_ref[...], sh_ref[...], sc_ref[...]).astype(bf16)
    z = jnp.dot(a, w_ref[...], preferred_element_type=f32)
    d = bg_ref.shape[-1]
    bg_ref[...] = z[:, :d].astype(bf16)
    cu_ref[...] = (z[:, d:2 * d] * z[:, 2 * d:]).astype(bf16)


def _lane_groups(x, op):
    parts = [x[:, i * LANES:(i + 1) * LANES] for i in range(x.shape[1] // LANES)]
    while len(parts) > 1:
        parts = [op(parts[i], parts[i + 1]) if i + 1 < len(parts) else parts[i]
                 for i in range(0, len(parts), 2)]
    return parts[0]


def _attn_body(q_ref, k_ref, v_ref, o_ref, s_scr, *, n_ctx, n_all, heads_per_step, group, dk, dv):
    def run(nk):
        chunks = [(a, min(a + ATTN_KEY_CHUNK, nk)) for a in range(0, nk, ATTN_KEY_CHUNK)]
        for j in range(heads_per_step):
            kv = j // group
            q = q_ref[:, j * dk:(j + 1) * dk]
            m = None
            for a, b in chunks:
                s = lax.dot_general(q, k_ref[a:b, kv * dk:(kv + 1) * dk], (((1,), (1,)), ((), ())),
                                    preferred_element_type=f32)
                s_scr[j, :, a:b] = s
                mc = _lane_groups(s, jnp.maximum)
                m = mc if m is None else jnp.maximum(m, mc)
            m = jnp.max(m, axis=-1, keepdims=True)
            o = l = None
            for a, b in chunks:
                p = jnp.exp2(s_scr[j, :, a:b] - m)
                lc = _lane_groups(p, jnp.add)
                oc = jnp.dot(p.astype(bf16), v_ref[a:b, kv * dv:(kv + 1) * dv], preferred_element_type=f32)
                o, l = (oc, lc) if o is None else (o + oc, l + lc)
            l = jnp.sum(l, axis=-1, keepdims=True)
            o_ref[:, j * dv:(j + 1) * dv] = (o / l).astype(o_ref.dtype)

    i = pl.program_id(2)
    pl.when(i == 0)(lambda: run(n_ctx))
    pl.when(i > 0)(lambda: run(n_all))


def _attn_call(q, k, v, n_batch, heads, kv_heads, dk, dv, n_ctx):
    nt = q.shape[0] // n_batch
    g = heads // kv_heads
    hp = ATTN_HEADS_PER_STEP
    kh = hp // g
    q3, k3, v3 = (a.reshape(n_batch, nt, a.shape[-1]) for a in (q, k, v))
    out = pl.pallas_call(
        functools.partial(_attn_body, n_ctx=n_ctx, n_all=nt, heads_per_step=hp, group=g, dk=dk, dv=dv),
        grid=(n_batch, heads // hp, nt // TM),
        in_specs=[pl.BlockSpec((None, TM, hp * dk), lambda b, h, i: (b, i, h)),
                  pl.BlockSpec((None, nt, kh * dk), lambda b, h, i: (b, 0, h)),
                  pl.BlockSpec((None, nt, kh * dv), lambda b, h, i: (b, 0, h))],
        out_specs=pl.BlockSpec((None, TM, hp * dv), lambda b, h, i: (b, i, h)),
        out_shape=jax.ShapeDtypeStruct((n_batch, nt, heads * dv), bf16),
        scratch_shapes=[pltpu.VMEM((hp, TM, nt), f32)],
        compiler_params=_params("arbitrary", "arbitrary", "arbitrary"),
        name="attention",
    )(q3, k3, v3)
    return out.reshape(n_batch * nt, heads * dv)


def _route_epilogue(x, y, g1_ref, gn2_ref, sh2_ref, sc2_ref, wr_ref, br_ref, ltri_ref, upper_ref,
                    xo_ref, f_ref, meta_ref, wts_ref, runs_ref, cnt_ref, carry_ref, lg_scr):
    i = pl.program_id(0)

    @pl.when(i == 0)
    def _():
        carry_ref[...] = jnp.zeros_like(carry_ref)
        lg_scr[...] = jnp.zeros_like(lg_scr)

    xn = x + g1_ref[...] * y
    xo_ref[...] = xn
    f = _modulate(xn, gn2_ref[...], sh2_ref[...], sc2_ref[...])
    f_hi = f.astype(bf16)
    f_ref[...] = f_hi
    f_lo = (f - f_hi.astype(f32)).astype(bf16)
    logits = lg_scr[(i + 1) % 2]
    lg_scr[i % 2] = (jnp.dot(f_hi, wr_ref[0], preferred_element_type=f32)
                     + jnp.dot(f_lo, wr_ref[0], preferred_element_type=f32)
                     + jnp.dot(f_hi, wr_ref[1], preferred_element_type=f32))
    routed = (i > 0).astype(f32)
    bias = br_ref[...]

    def pick(onehot, v):
        return jnp.sum(jnp.where(onehot, v, 0.0), axis=-1, keepdims=True)

    def route_rows(lg):
        lane = lax.broadcasted_iota(jnp.int32, lg.shape, 1).astype(f32)

        def first_argmax(score):
            top = jnp.max(score, axis=-1, keepdims=True)
            return jnp.min(jnp.where(score == top, lane, float(LANES)), axis=-1, keepdims=True)

        def masked_softmax(mask):
            z = jnp.where(mask, lg, NEG)
            e = jnp.exp(z - jnp.max(z, axis=-1, keepdims=True))
            return e / jnp.sum(e, axis=-1, keepdims=True)

        gmask = lane < float(MOE_GROUPS)
        gp = masked_softmax(gmask)
        g_sel = first_argmax(jnp.where(gmask, gp + bias, NEG))
        g_w = pick(lane == g_sel, gp)
        lo = float(ROUTE_LANE0) + float(MOE_EXPERTS_PER_GROUP) * g_sel
        emask = (lane >= lo) & (lane < lo + float(MOE_EXPERTS_PER_GROUP))
        ep = masked_softmax(emask)
        escore = jnp.where(emask, ep + bias, NEG)
        i1 = first_argmax(escore)
        oh1 = lane == i1
        i2 = first_argmax(jnp.where(oh1, NEG, escore))
        oh2 = lane == i2
        p1, p2 = pick(oh1, ep), pick(oh2, ep)
        return (jnp.where(oh1, 1.0, 0.0), jnp.where(oh2, 1.0, 0.0),
                p1 / (p1 + p2) * g_w, p2 / (p1 + p2) * g_w)

    m1, m2, w1, w2 = route_rows(logits)
    oh1, oh2 = m1 > 0.5, m2 > 0.5
    lane = lax.broadcasted_iota(jnp.int32, logits.shape, 1).astype(f32)

    hits = (m1 + m2) * routed
    before = jnp.dot(ltri_ref[...], hits.astype(bf16), preferred_element_type=f32)
    n_tile = jnp.sum(hits, axis=0, keepdims=True)
    n_al = jnp.floor((n_tile + float(RUN_ALIGN - 1)) * (1.0 / RUN_ALIGN)) * float(RUN_ALIGN)
    lo = jnp.dot(jnp.broadcast_to(n_al, (SUBLANES, LANES)), upper_ref[...], preferred_element_type=f32,
                 precision=lax.Precision.HIGHEST)[0:1, :]
    lp1, lp2 = pick(oh1, before + lo), pick(oh2, before + lo)

    row = lax.broadcasted_iota(jnp.int32, runs_ref.shape, 0)
    runs = jnp.where(row == 0, n_tile, jnp.where(row == 1, carry_ref[...], jnp.where(row == 2, lo, 0.0)))
    runs_ref[...] = runs.astype(jnp.int32)
    carry_ref[...] = carry_ref[...] + n_tile
    cnt_ref[...] = carry_ref[...]

    meta = jnp.where(lane == 0.0, lp1, jnp.where(lane == 1.0, lp2, 0.0))
    meta_ref[...] = meta.T[:meta_ref.shape[0], :].astype(jnp.int32)
    col = lax.broadcasted_iota(jnp.int32, wts_ref.shape, 1)
    wts_ref[...] = jnp.where(col == 0, w1, jnp.where(col == 1, w2, jnp.where(col == 2, lp1, lp2)))


def _attn_out_body(y_ref, wo_ref, x_ref, *rest):
    y = jnp.dot(y_ref[...], wo_ref[...], preferred_element_type=f32)
    _route_epilogue(x_ref[...], y, *rest)


def _conv_out_body(cu_ref, prev_ref, next_ref, bg_ref, cw_ref, wo_ref, x_ref, *rest, tiles_per_b):
    t = jnp.minimum(pl.program_id(0), pl.num_programs(0) - 2) % tiles_per_b
    has_prev = (t >= 2).astype(f32)
    has_next = ((t != 0) & (t != tiles_per_b - 1)).astype(f32)
    cu = cu_ref[...].astype(f32)
    row = lax.broadcasted_iota(jnp.int32, cu.shape, 0)
    hp = prev_ref[HALO - 1:HALO, :].astype(f32) * has_prev
    hn = next_ref[0:1, :].astype(f32) * has_next
    up = jnp.where(row == 0, hp, pltpu.roll(cu, 1, 0))
    dn = jnp.where(row == TM - 1, hn, pltpu.roll(cu, TM - 1, 0))
    cw = cw_ref[...]
    z = cw[0:1, :] * up + cw[1:2, :] * cu + cw[2:3, :] * dn
    yb = (bg_ref[...].astype(f32) * z).astype(bf16)
    y = jnp.dot(yb, wo_ref[...], preferred_element_type=f32)
    _route_epilogue(x_ref[...], y, *rest)


def _mixer_tile_spec(n, last):
    return pl.BlockSpec((TM, n), lambda i: (jnp.minimum(i, last), 0))


def _route_specs(d, tiles_per_b, n_batch):
    last = tiles_per_b * n_batch - 1

    def routed(i):
        return jnp.maximum(i - 1, 0)

    ins = [_mod_spec(2, tiles_per_b, n_batch), _row_spec(d), _mod_spec(3, tiles_per_b, n_batch),
           _mod_spec(4, tiles_per_b, n_batch), _full_spec((2, d, LANES)), _row_spec(LANES),
           _full_spec((TM, TM)), _full_spec((LANES, LANES))]
    outs = [_mixer_tile_spec(d, last), _mixer_tile_spec(d, last),
            pl.BlockSpec((None, META_ROWS, TM), lambda i: (routed(i), 0, 0)),
            pl.BlockSpec((TM, 8), lambda i: (routed(i), 0)),
            pl.BlockSpec((None, SUBLANES, LANES), lambda i: (routed(i), 0, 0)), _row_spec(LANES)]
    return ins, outs


def _route_out_shapes(t, d):
    return [jax.ShapeDtypeStruct((t, d), f32), jax.ShapeDtypeStruct((t, d), bf16),
            jax.ShapeDtypeStruct((t // TM, META_ROWS, TM), jnp.int32), jax.ShapeDtypeStruct((t, 8), f32),
            jax.ShapeDtypeStruct((t // TM, SUBLANES, LANES), jnp.int32),
            jax.ShapeDtypeStruct((1, LANES), f32)]


def _rows_to_tiles(tref, x):
    n = x.shape[0]
    for s in range(SUBLANES):
        tref[pl.ds(s, n, stride=SUBLANES), :] = x[:, s * LANES:(s + 1) * LANES]


def _tiles_to_rows(tref):
    n = tref.shape[0] // SUBLANES
    return jnp.concatenate([tref[pl.ds(s, n, stride=SUBLANES), :] for s in range(SUBLANES)], axis=-1)


def _tile_rows(ref, row, n):
    return ref.at[pl.ds(row * SUBLANES, n * SUBLANES)]


def _run_copies(chunks_ref, copy):
    n_chunks = chunks_ref[1, 0]

    def chunk(q, carry):
        copy(q * RUN_ALIGN, chunks_ref[0, q]).start()
        return carry

    lax.fori_loop(0, n_chunks, chunk, 0)
    return n_chunks


def _wait_chunks(n_chunks, copy):
    def wait_one(c, carry):
        copy(0, 0).wait()
        return carry
    lax.fori_loop(0, n_chunks, wait_one, 0)


def _dispatch_body(seg_ref, chunks_ref, meta_ref, f_ref, xp_ref, sbuf, zbuf, pending, sem, zsem):
    i = pl.program_id(0)
    n_blocks = xp_ref.shape[0] // (EXPERT_BLOCK * SUBLANES)

    @pl.when(i == 0)
    def _():
        zbuf[...] = jnp.zeros_like(zbuf)

        def pad_fill(e, wait):
            n_pad = seg_ref[2 * MOE_EXPERTS + e]
            off = seg_ref[MOE_EXPERTS + e]
            for piece in PAD_PIECES:
                take = (n_pad & piece) != 0
                cp = pltpu.make_async_copy(_tile_rows(zbuf, 0, piece), _tile_rows(xp_ref, off, piece), zsem)
                pl.when(take)(cp.wait if wait else cp.start)
                off = off + jnp.where(take, piece, 0)

        def tail_fill(b, wait):
            cp = pltpu.make_async_copy(zbuf, _tile_rows(xp_ref, b * EXPERT_BLOCK, EXPERT_BLOCK), zsem)
            cp.wait() if wait else cp.start()

        for wait in (False, True):
            lax.fori_loop(0, MOE_EXPERTS, lambda e, c, w=wait: (pad_fill(e, w), c)[1], 0)
            lax.fori_loop(seg_ref[3 * MOE_EXPERTS], n_blocks, lambda b, c, w=wait: (tail_fill(b, w), c)[1], 0)

        pending[0] = 0

    buf = sbuf.at[i % 2]
    pos = lax.broadcasted_iota(jnp.int32, (LOCAL_ROWS, TM), 0)
    m = meta_ref[...]
    perm = jnp.where((pos == m[0:1, :]) | (pos == m[1:2, :]), 1.0, 0.0).astype(bf16)
    _rows_to_tiles(buf, jnp.dot(perm, f_ref[...], preferred_element_type=f32))

    def copy(local_row, slot_row):
        return pltpu.make_async_copy(_tile_rows(buf, local_row, RUN_ALIGN),
                                     _tile_rows(xp_ref, slot_row, RUN_ALIGN), sem)

    _wait_chunks(pending[0], copy)
    pending[0] = _run_copies(chunks_ref, copy)

    @pl.when(i == pl.num_programs(0) - 1)
    def _():
        _wait_chunks(pending[0], copy)


def _dispatch_call(seg, chunks, meta, f, n_rows):
    t, d = f.shape
    grid_spec = pltpu.PrefetchScalarGridSpec(
        num_scalar_prefetch=1,
        grid=(t // TM,),
        in_specs=[pl.BlockSpec((None, SUBLANES, LANES), lambda i, seg: (i, 0, 0), memory_space=pltpu.SMEM),
                  pl.BlockSpec((None, META_ROWS, TM), lambda i, seg: (i, 0, 0)),
                  pl.BlockSpec((TM, d), lambda i, seg: (i, 0))],
        out_specs=pl.BlockSpec(memory_space=pl.ANY),
        scratch_shapes=[pltpu.VMEM((2, LOCAL_ROWS * SUBLANES, LANES), f32),
                        pltpu.VMEM((EXPERT_BLOCK * SUBLANES, LANES), f32),
                        pltpu.SMEM((1,), jnp.int32),
                        pltpu.SemaphoreType.DMA(()), pltpu.SemaphoreType.DMA(())],
    )
    return pl.pallas_call(
        _dispatch_body,
        grid_spec=grid_spec,
        out_shape=jax.ShapeDtypeStruct((n_rows * SUBLANES, LANES), f32),
        compiler_params=_params("arbitrary"),
        name="moe_dispatch",
    )(seg, chunks, meta, f)


def _expert_body(be_ref, na_ref, x_ref, wg_hbm, wu_hbm, wd_hbm, o_ref,
                 wg_f, wu_f, wd_f, wg_s, wu_s, wd_s, wsem, *, layer):
    i = pl.program_id(0)
    active = i < na_ref[0]
    e = be_ref[i]
    changed = (i == 0) | (e != be_ref[jnp.maximum(i - 1, 0)])

    def fetch(expert):
        pairs = ((wg_hbm, wg_f), (wu_hbm, wu_f), (wd_hbm, wd_f))
        return [pltpu.make_async_copy(w.at[layer, expert], buf, wsem.at[j]) for j, (w, buf) in enumerate(pairs)]

    @pl.when(i == 0)
    def _():
        for cp in fetch(e):
            cp.start()

    @pl.when(active & changed)
    def _():
        for cp in fetch(e):
            cp.wait()
        wg_s[...] = wg_f[...].astype(bf16)
        wu_s[...] = wu_f[...].astype(bf16)
        wd_s[...] = wd_f[...].astype(bf16)

        @pl.when(e + 1 < MOE_EXPERTS)
        def _():
            for cp in fetch(e + 1):
                cp.start()

    @pl.when(active)
    def _():
        x = _tiles_to_rows(x_ref).astype(bf16)
        g = jnp.dot(x, wg_s[...], preferred_element_type=f32)
        u = jnp.dot(x, wu_s[...], preferred_element_type=f32)
        h = (_silu(g) * u).astype(bf16)
        _rows_to_tiles(o_ref, jnp.dot(h, wd_s[...], preferred_element_type=f32))

    @pl.when(jnp.logical_not(active))
    def _():
        o_ref[...] = jnp.zeros_like(o_ref)


def _expert_call(blk_e, n_act, xp, w_gate, w_up, w_down, layer):
    n_rows = xp.shape[0] // SUBLANES
    d, ff = w_gate.shape[-2:]
    blk_rows = EXPERT_BLOCK * SUBLANES

    def blk(i, be, na):
        return jnp.maximum(jnp.minimum(i, na[0] - 1), 0)

    any_spec = pl.BlockSpec(memory_space=pl.ANY)
    grid_spec = pltpu.PrefetchScalarGridSpec(
        num_scalar_prefetch=2,
        grid=(n_rows // EXPERT_BLOCK,),
        in_specs=[pl.BlockSpec((blk_rows, LANES), lambda i, be, na: (blk(i, be, na), 0)),
                  any_spec, any_spec, any_spec],
        out_specs=pl.BlockSpec((blk_rows, LANES), lambda i, be, na: (i, 0)),
        scratch_shapes=[pltpu.VMEM((d, ff), f32), pltpu.VMEM((d, ff), f32), pltpu.VMEM((ff, d), f32),
                        pltpu.VMEM((d, ff), bf16), pltpu.VMEM((d, ff), bf16), pltpu.VMEM((ff, d), bf16),
                        pltpu.SemaphoreType.DMA((3,))],
    )
    return pl.pallas_call(
        functools.partial(_expert_body, layer=layer),
        grid_spec=grid_spec,
        out_shape=jax.ShapeDtypeStruct((n_rows * SUBLANES, LANES), f32),
        compiler_params=_params("arbitrary"),
        name="moe_experts",
    )(blk_e, n_act, xp, w_gate, w_up, w_down)


def _combine_body(chunks_ref, next_chunks_ref, x_ref, wts_ref, g2_ref, fg_ref, yp_ref, o_ref,
                  gbuf, pending, sem, *, final):
    step = pl.program_id(0) * pl.num_programs(1) + pl.program_id(1)
    n_steps = pl.num_programs(0) * pl.num_programs(1)
    cur = step % 2

    def fetch(buf):
        def copy(local_row, slot_row):
            return pltpu.make_async_copy(_tile_rows(yp_ref, slot_row, RUN_ALIGN),
                                         _tile_rows(gbuf.at[buf], local_row, RUN_ALIGN), sem.at[buf])
        return copy

    @pl.when(step == 0)
    def _():
        gbuf[...] = jnp.zeros_like(gbuf)
        pending[0] = _run_copies(chunks_ref, fetch(0))

    @pl.when(step + 1 < n_steps)
    def _():
        pending[1 - cur] = _run_copies(next_chunks_ref, fetch(1 - cur))

    _wait_chunks(pending[cur], fetch(cur))
    ys = _tiles_to_rows(gbuf.at[cur]).astype(bf16)
    w = wts_ref[...]
    pos = lax.broadcasted_iota(jnp.int32, (TM, LOCAL_ROWS), 1).astype(f32)
    y = None
    for k in range(MOE_TOP_K):
        pick = jnp.where(pos == w[:, MOE_TOP_K + k:MOE_TOP_K + k + 1], 1.0, 0.0).astype(bf16)
        yk = w[:, k:k + 1] * jnp.dot(pick, ys, preferred_element_type=f32)
        y = yk if y is None else y + yk
    xn = x_ref[...] + g2_ref[...] * y
    if final:
        xn = _rms(xn) * fg_ref[...]
    o_ref[...] = xn


def _combine_call(chunks, x, wts, mods3, final_g, yp, n_batch, tiles_per_b, final):
    t, d = x.shape
    t0 = 1 if final else 0
    nt = tiles_per_b - t0

    def tile(b, j):
        return b * tiles_per_b + t0 + j

    def next_tile(b, j):
        nxt = jnp.minimum(b * nt + j + 1, n_batch * nt - 1)
        return tile(nxt // nt, nxt % nt)

    def g2_map(b, j):
        r = jnp.where(t0 + j == 0, n_batch, b)
        return (r * N_MOD + 5, 0, 0)

    return pl.pallas_call(
        functools.partial(_combine_body, final=final),
        grid=(n_batch, nt),
        in_specs=[pl.BlockSpec((None, SUBLANES, LANES), lambda b, j: (tile(b, j), 0, 0),
                               memory_space=pltpu.SMEM),
                  pl.BlockSpec((None, SUBLANES, LANES), lambda b, j: (next_tile(b, j), 0, 0),
                               memory_space=pltpu.SMEM),
                  pl.BlockSpec((TM, d), lambda b, j: (tile(b, j), 0)),
                  pl.BlockSpec((TM, 8), lambda b, j: (tile(b, j), 0)),
                  pl.BlockSpec((None, 1, d), g2_map),
                  pl.BlockSpec((1, d), lambda b, j: (0, 0)),
                  pl.BlockSpec(memory_space=pl.ANY)],
        out_specs=pl.BlockSpec((TM, d), lambda b, j: (b * nt + j, 0)),
        scratch_shapes=[pltpu.VMEM((2, LOCAL_ROWS * SUBLANES, LANES), f32), pltpu.SMEM((2,), jnp.int32),
                        pltpu.SemaphoreType.DMA((2,))],
        out_shape=jax.ShapeDtypeStruct((n_batch * nt * TM, d), f32),
        compiler_params=_params("arbitrary", "arbitrary"),
        name="moe_combine",
    )(chunks, chunks, x, wts, mods3, final_g, yp)


def _chunk_table(runs, pstarts):
    n_chunk_lanes = LOCAL_ROWS // RUN_ALIGN
    assert n_chunk_lanes <= LANES
    experts = slice(ROUTE_LANE0, ROUTE_LANE0 + MOE_EXPERTS)
    n, base, lo = runs[:, 0, experts], runs[:, 1, experts], runs[:, 2, experts]
    ends = lo + (n + RUN_ALIGN - 1) // RUN_ALIGN * RUN_ALIGN
    row0 = jnp.arange(LANES, dtype=jnp.int32) * RUN_ALIGN
    owner = jnp.sum((ends[:, None, :] <= row0[None, :, None]).astype(jnp.int32), axis=-1)
    shift = pstarts[None, :] + base - lo
    mine = owner[:, :, None] == jnp.arange(MOE_EXPERTS, dtype=jnp.int32)[None, None, :]
    slot = row0[None, :] + jnp.sum(jnp.where(mine, shift[:, None, :], 0), axis=-1)
    count = jnp.broadcast_to(ends[:, -1:] // RUN_ALIGN, slot.shape)
    rest = jnp.zeros((runs.shape[0], SUBLANES - 2, LANES), jnp.int32)
    return jnp.concatenate([slot[:, None, :], count[:, None, :], rest], axis=1).astype(jnp.int32)
def _rope_tables(seq, n_ctx, half):
    inv = ROPE_THETA ** (-jnp.arange(half, dtype=f32) / half)
    t = jnp.arange(seq, dtype=jnp.int32)
    ang_r = (t // GRID_W).astype(f32)[:, None] * inv[None, :]
    ang_c = (t % GRID_W).astype(f32)[:, None] * inv[None, :]
    gap = jnp.zeros((seq, LANES // 2 - 2 * half), f32)
    cos_h = jnp.concatenate([jnp.cos(ang_r), jnp.cos(ang_c), gap], axis=-1)
    sin_h = jnp.concatenate([jnp.sin(ang_r), jnp.sin(ang_c), gap], axis=-1)
    cos = jnp.concatenate([cos_h, cos_h], axis=-1)
    sin = jnp.concatenate([-sin_h, sin_h], axis=-1)
    ctx_c = jnp.broadcast_to((cos[:1] != 0.0).astype(f32), (n_ctx, LANES))
    ctx_s = jnp.zeros((n_ctx, LANES), f32)
    return jnp.concatenate([ctx_c, cos], axis=0), jnp.concatenate([ctx_s, sin], axis=0)


def _rope_lanes(w, half):
    lead = w.shape[:-1]
    halves = w.reshape(lead + (2, 2, half)).swapaxes(-3, -2).reshape(lead + (2, 2 * half))
    pad = [(0, 0)] * (len(lead) + 1) + [(0, LANES // 2 - 2 * half)]
    return jnp.pad(halves, pad).reshape(lead + (LANES,))


def _table_spec(tiles_per_b):
    return pl.BlockSpec((TM, LANES), lambda i: (i % tiles_per_b, 0))


def kernel(x, c, ctx, c_ctx, w_mod, b_mod, norm_mix_g, norm_ffn_g, mla_w_dq, mla_g_q, mla_w_uq, mla_w_dkv, mla_g_kv, mla_w_ukv, mla_w_o, conv_w_in, conv_w, conv_w_out, gqa_w_qkv, gqa_q_norm_g, gqa_k_norm_g, gqa_w_o, moe_w_group, moe_b_group, moe_w_expert, moe_b_expert, moe_w_gate, moe_w_up, moe_w_down, final_norm_g):
    n_batch, seq, d = x.shape
    n_ctx = ctx.shape[1]
    depth = w_mod.shape[0]
    assert n_ctx == TM and seq % TM == 0 and d == SUBLANES * LANES
    nt = n_ctx + seq
    tiles_per_b = nt // TM
    t_all = n_batch * nt
    n_tiles = t_all // TM
    n_assign = t_all * MOE_TOP_K
    n_blocks = -(-(n_assign + MOE_EXPERTS * (RUN_ALIGN - 1)) // EXPERT_BLOCK) + MOE_EXPERTS
    n_rows = n_blocks * EXPERT_BLOCK

    xs = jnp.concatenate([ctx, x], axis=1).reshape(t_all, d)
    cond8 = jnp.concatenate([c, c_ctx[None], jnp.zeros((8 - n_batch - 1, d), f32)], axis=0)
    mods = _mod_call(cond8, w_mod, b_mod)
    mods = mods.reshape(depth, 8 * N_MOD, 1, d)

    mla_tabs = _rope_tables(seq, n_ctx, MLA_ROPE // 4)
    gqa_tabs = _rope_tables(seq, n_ctx, GQA_HEAD_DIM // 4)
    ltri = jnp.tril(jnp.ones((TM, TM), f32), -1).astype(bf16)
    upper = jnp.triu(jnp.ones((LANES, LANES), f32), 1)
    mspec = functools.partial(_mod_spec, tiles_per_b=tiles_per_b, n_batch=n_batch)
    tab_specs = [_table_spec(tiles_per_b)] * 2
    route_in, route_out = _route_specs(d, tiles_per_b, n_batch)
    route_shapes = _route_out_shapes(t_all, d)
    route_scratch = [pltpu.VMEM((1, LANES), f32), pltpu.VMEM((2, TM, LANES), f32)]
    mtile = functools.partial(_mixer_tile_spec, last=n_tiles - 1)

    for i in range(depth):
        kind, j = i % N_MIXERS, i // N_MIXERS
        m3 = mods[i]
        gmix = norm_mix_g[i][None]
        head_in = [_tile_spec(d), mspec(0), mspec(1), _row_spec(d)]
        wr = jnp.pad(jnp.concatenate([moe_w_group[i], moe_w_expert[i]], axis=1),
                     ((0, 0), (0, LANES - MOE_GROUPS - MOE_EXPERTS)))
        wr_hi = wr.astype(bf16)
        wr = jnp.stack([wr_hi, (wr - wr_hi.astype(f32)).astype(bf16)])
        br = jnp.pad(jnp.concatenate([moe_b_group[i], moe_b_expert[i]]),
                     (0, LANES - MOE_GROUPS - MOE_EXPERTS))[None]
        route_args = (m3, norm_ffn_g[i][None], m3, m3, wr, br, ltri, upper)

        if kind == 0:
            dk, dv = 2 * LANES, MLA_V
            half = MLA_ROPE // 4
            wcat = jnp.concatenate([mla_w_dq[j], mla_w_dkv[j][:, :MLA_KV_LORA],
                                    _rope_lanes(mla_w_dkv[j][:, MLA_KV_LORA:], half)], axis=1).astype(bf16)
            wuq3 = mla_w_uq[j].reshape(MLA_Q_LORA, MLA_HEADS, MLA_NOPE + MLA_ROPE)
            wuq = jnp.concatenate([wuq3[..., :MLA_NOPE], _rope_lanes(wuq3[..., MLA_NOPE:], half)], axis=-1
                                  ).reshape(MLA_Q_LORA, MLA_HEADS * dk).astype(bf16)
            wukv3 = mla_w_ukv[j].reshape(MLA_KV_LORA, MLA_HEADS, MLA_NOPE + MLA_V)
            wukv = jnp.concatenate([wukv3[..., :MLA_NOPE].reshape(MLA_KV_LORA, -1),
                                    wukv3[..., MLA_NOPE:].reshape(MLA_KV_LORA, -1)], axis=1).astype(bf16)
            q, k, v = pl.pallas_call(
                functools.partial(_mla_proj_body, scale=(MLA_NOPE + MLA_ROPE) ** -0.5 * LOG2E),
                grid=(n_tiles,),
                in_specs=head_in + [_full_spec(wcat.shape), _row_spec(MLA_Q_LORA), _row_spec(MLA_KV_LORA),
                                    _full_spec(wuq.shape), _full_spec(wukv.shape)] + tab_specs,
                out_specs=[_tile_spec(MLA_HEADS * dk), _tile_spec(MLA_HEADS * dk), _tile_spec(MLA_HEADS * dv)],
                out_shape=[jax.ShapeDtypeStruct((t_all, MLA_HEADS * dk), bf16),
                           jax.ShapeDtypeStruct((t_all, MLA_HEADS * dk), bf16),
                           jax.ShapeDtypeStruct((t_all, MLA_HEADS * dv), bf16)],
                compiler_params=_params("arbitrary"),
                name="mla_proj",
            )(xs, m3, m3, gmix, wcat, mla_g_q[j][None], mla_g_kv[j][None], wuq, wukv, *mla_tabs)
            att = _attn_call(q, k, v, n_batch, MLA_HEADS, MLA_HEADS, dk, dv, n_ctx)
            w_o = mla_w_o[j].astype(bf16)
        elif kind == 2:
            dk = dv = GQA_HEAD_DIM
            half = GQA_HEAD_DIM // 4
            n_qk = GQA_HEADS + GQA_KV_HEADS
            wqk = _rope_lanes(gqa_w_qkv[j][:, :n_qk * dk].reshape(d, n_qk, dk), half).reshape(d, n_qk * dk)
            wqkv = jnp.concatenate([wqk, gqa_w_qkv[j][:, n_qk * dk:]], axis=1).astype(bf16)
            q, k, v = pl.pallas_call(
                functools.partial(_gqa_proj_body, scale=GQA_HEAD_DIM ** -0.5 * LOG2E),
                grid=(n_tiles,),
                in_specs=head_in + [_full_spec(wqkv.shape), _row_spec(dk), _row_spec(dk)] + tab_specs,
                out_specs=[_tile_spec(GQA_HEADS * dk), _tile_spec(GQA_KV_HEADS * dk),
                           _tile_spec(GQA_KV_HEADS * dv)],
                out_shape=[jax.ShapeDtypeStruct((t_all, GQA_HEADS * dk), bf16),
                           jax.ShapeDtypeStruct((t_all, GQA_KV_HEADS * dk), bf16),
                           jax.ShapeDtypeStruct((t_all, GQA_KV_HEADS * dv), bf16)],
                compiler_params=_params("arbitrary"),
                name="gqa_proj",
            )(xs, m3, m3, gmix, wqkv, _rope_lanes(gqa_q_norm_g[j][None], half),
              _rope_lanes(gqa_k_norm_g[j][None], half), *gqa_tabs)
            att = _attn_call(q, k, v, n_batch, GQA_HEADS, GQA_KV_HEADS, dk, dv, n_ctx)
            w_o = gqa_w_o[j].astype(bf16)

        if kind == 1:
            w_in = conv_w_in[j].astype(bf16)
            bg, cu = pl.pallas_call(
                _conv_proj_body,
                grid=(n_tiles,),
                in_specs=head_in + [_full_spec(w_in.shape)],
                out_specs=[_tile_spec(d), _tile_spec(d)],
                out_shape=[jax.ShapeDtypeStruct((t_all, d), bf16)] * 2,
                compiler_params=_params("arbitrary"),
                name="conv_proj",
            )(xs, m3, m3, gmix, w_in)
            w_o = conv_w_out[j].astype(bf16)
            per = TM // HALO
            last = t_all // HALO - 1
            xs, f, meta, wts, runs, cnt = pl.pallas_call(
                functools.partial(_conv_out_body, tiles_per_b=tiles_per_b),
                grid=(n_tiles + 1,),
                in_specs=[mtile(d),
                          pl.BlockSpec((HALO, d), lambda i: (
                              jnp.maximum(jnp.minimum(i, n_tiles - 1) * per - 1, 0), 0)),
                          pl.BlockSpec((HALO, d), lambda i: (jnp.minimum((i + 1) * per, last), 0)),
                          mtile(d), _full_spec((3, d)), _full_spec(w_o.shape), mtile(d)] + route_in,
                out_specs=route_out,
                out_shape=route_shapes,
                scratch_shapes=route_scratch,
                compiler_params=_params("arbitrary"),
                name="conv_out_route",
            )(cu, cu, cu, bg, conv_w[j], w_o, xs, *route_args)
        else:
            xs, f, meta, wts, runs, cnt = pl.pallas_call(
                _attn_out_body,
                grid=(n_tiles + 1,),
                in_specs=[mtile(att.shape[-1]), _full_spec(w_o.shape), mtile(d)] + route_in,
                out_specs=route_out,
                out_shape=route_shapes,
                scratch_shapes=route_scratch,
                compiler_params=_params("arbitrary"),
                name="attn_out_route",
            )(att, w_o, xs, *route_args)

        counts = cnt[0, ROUTE_LANE0:ROUTE_LANE0 + MOE_EXPERTS].astype(jnp.int32)
        padded = (counts + RUN_ALIGN - 1 + EXPERT_BLOCK - 1) // EXPERT_BLOCK * EXPERT_BLOCK
        pends = jnp.cumsum(padded)
        pstarts = pends - padded
        blk_row0 = jnp.arange(n_blocks, dtype=jnp.int32) * EXPERT_BLOCK
        blk_e = jnp.minimum(jnp.sum((pends[None, :] <= blk_row0[:, None]).astype(jnp.int32), axis=1),
                            MOE_EXPERTS - 1)
        n_act = (pends[-1:] // EXPERT_BLOCK).astype(jnp.int32)
        seg = jnp.concatenate([pstarts, pstarts + counts, padded - counts, n_act,
                               jnp.zeros((MOE_EXPERTS - 1,), jnp.int32)]).astype(jnp.int32)

        chunks = _chunk_table(runs, pstarts)

        xp = _dispatch_call(seg, chunks, meta, f, n_rows)
        yp = _expert_call(blk_e, n_act, xp, moe_w_gate, moe_w_up, moe_w_down, i)
        final = i == depth - 1
        xs = _combine_call(chunks, xs, wts, m3, final_norm_g[None], yp, n_batch, tiles_per_b, final)

    return xs.reshape(n_batch, seq, d)
```

```python
import functools

import jax
import jax.numpy as jnp
from jax import lax
from jax.experimental import pallas as pl
from jax.experimental.pallas import tpu as pltpu

f32 = jnp.float32
bf16 = jnp.bfloat16

GRID_W = 64
ROPE_THETA = 10000.0
NORM_EPS = 1e-6
N_MOD = 6
N_MIXERS = 3
MLA_HEADS, MLA_Q_LORA, MLA_KV_LORA, MLA_NOPE, MLA_ROPE, MLA_V = 8, 512, 256, 128, 64, 128
GQA_HEADS, GQA_KV_HEADS, GQA_HEAD_DIM = 16, 8, 128
MOE_GROUPS, MOE_EXPERTS_PER_GROUP, MOE_TOP_K = 4, 8, 2
MOE_EXPERTS = MOE_GROUPS * MOE_EXPERTS_PER_GROUP

LANES = 128
TM = 256
EXPERT_BLOCK = 256
SUBLANES = 8
RUN_ALIGN = 8
WAIT_GROUP = 8
PAD_PIECES = tuple(1 << b for b in reversed(range(EXPERT_BLOCK.bit_length())))
HALO = 16
ATTN_KEY_CHUNK = 256
ATTN_HEADS_PER_STEP = 4
LOG2E = 1.4426950408889634
VMEM_LIMIT = 56 * 1024 * 1024
ROUTE_LANE0 = MOE_GROUPS
META_ROWS = 8
MXU_TILE = 256
LOCAL_ROWS = -(-(TM * MOE_TOP_K + MOE_EXPERTS * (RUN_ALIGN - 1)) // MXU_TILE) * MXU_TILE
NEG = -1e30


def _params(*sem):
    return pltpu.CompilerParams(dimension_semantics=sem, vmem_limit_bytes=VMEM_LIMIT)


def _rms(x):
    return x * lax.rsqrt(jnp.mean(x * x, axis=-1, keepdims=True) + NORM_EPS)


def _modulate(x, gn, sh, sc):
    return _rms(x) * gn * (1.0 + sc) + sh


def _silu(x):
    return x / (1.0 + jnp.exp(-x))


def _rope(x, c, s):
    return x * c + pltpu.roll(x, LANES // 2, 1) * s


def _mod_body(cond_ref, w_ref, b_ref, o_ref):
    a = _silu(cond_ref[...])
    o_ref[...] = jnp.dot(a, w_ref[...], preferred_element_type=f32,
                         precision=lax.Precision.HIGHEST) + b_ref[...]


def _mod_call(cond8, w_mod, b_mod):
    depth, d, n6 = w_mod.shape
    tn = n6 // 4
    return pl.pallas_call(
        _mod_body,
        grid=(depth, n6 // tn),
        in_specs=[pl.BlockSpec((8, d), lambda l, j: (0, 0)),
                  pl.BlockSpec((None, d, tn), lambda l, j: (l, 0, j)),
                  pl.BlockSpec((None, 1, tn), lambda l, j: (l, 0, j))],
        out_specs=pl.BlockSpec((None, 8, tn), lambda l, j: (l, 0, j)),
        out_shape=jax.ShapeDtypeStruct((depth, 8, n6), f32),
        compiler_params=_params("arbitrary", "arbitrary"),
        name="mod_vectors",
    )(cond8, w_mod, b_mod.reshape(depth, 1, n6))


def _mod_spec(chunk, tiles_per_b, n_batch, d=SUBLANES * LANES):
    def imap(i):
        i = jnp.minimum(i, tiles_per_b * n_batch - 1)
        r = jnp.where(i % tiles_per_b == 0, n_batch, i // tiles_per_b)
        return (r * N_MOD + chunk, 0, 0)
    return pl.BlockSpec((None, 1, d), imap)


def _row_spec(n):
    return pl.BlockSpec((1, n), lambda i: (0, 0))


def _full_spec(shape):
    return pl.BlockSpec(shape, lambda i: (0,) * len(shape))


def _tile_spec(n):
    return pl.BlockSpec((TM, n), lambda i: (i, 0))


def _mla_proj_body(x_ref, sh_ref, sc_ref, gn_ref, wcat_ref, gq_ref, gkv_ref, wuq_ref, wukv_ref,
                   c_ref, s_ref, q_ref, k_ref, v_ref, *, scale):
    a = _modulate(x_ref[...], gn_ref[...], sh_ref[...], sc_ref[...]).astype(bf16)
    z = jnp.dot(a, wcat_ref[...], preferred_element_type=f32)
    cq = (_rms(z[:, :MLA_Q_LORA]) * gq_ref[...]).astype(bf16)
    ckv = (_rms(z[:, MLA_Q_LORA:MLA_Q_LORA + MLA_KV_LORA]) * gkv_ref[...]).astype(bf16)
    kpe = z[:, MLA_Q_LORA + MLA_KV_LORA:]
    q = jnp.dot(cq, wuq_ref[...], preferred_element_type=f32)
    kv = jnp.dot(ckv, wukv_ref[...], preferred_element_type=f32)
    c, s = c_ref[...], s_ref[...]
    kpe_r = _rope(kpe, c, s).astype(bf16)
    hd = 2 * LANES
    for h in range(MLA_HEADS):
        q_ref[:, h * hd:h * hd + LANES] = (q[:, h * hd:h * hd + LANES] * scale).astype(bf16)
        q_ref[:, h * hd + LANES:(h + 1) * hd] = (
            _rope(q[:, h * hd + LANES:(h + 1) * hd], c, s) * scale).astype(bf16)
        k_ref[:, h * hd:h * hd + LANES] = kv[:, h * LANES:(h + 1) * LANES].astype(bf16)
        k_ref[:, h * hd + LANES:(h + 1) * hd] = kpe_r
    v_ref[...] = kv[:, MLA_HEADS * MLA_NOPE:].astype(bf16)


def _gqa_proj_body(x_ref, sh_ref, sc_ref, gn_ref, w_ref, qg_ref, kg_ref,
                   c_ref, s_ref, q_ref, k_ref, v_ref, *, scale):
    a = _modulate(x_ref[...], gn_ref[...], sh_ref[...], sc_ref[...]).astype(bf16)
    z = jnp.dot(a, w_ref[...], preferred_element_type=f32)
    c, s = c_ref[...], s_ref[...]
    dq = GQA_HEADS * GQA_HEAD_DIM
    dkv = GQA_KV_HEADS * GQA_HEAD_DIM
    qg = qg_ref[...] * scale
    kg = kg_ref[...]
    for h in range(GQA_HEADS):
        qh = _rms(z[:, h * LANES:(h + 1) * LANES]) * qg
        q_ref[:, h * LANES:(h + 1) * LANES] = _rope(qh, c, s).astype(bf16)
    for h in range(GQA_KV_HEADS):
        kh = _rms(z[:, dq + h * LANES:dq + (h + 1) * LANES]) * kg
        k_ref[:, h * LANES:(h + 1) * LANES] = _rope(kh, c, s).astype(bf16)
    v_ref[...] = z[:, dq + dkv:].astype(bf16)


def _conv_proj_body(x_ref, sh_ref, sc_ref, gn_ref, w_ref, bg_ref, cu_ref):
    a = _modulate(x_ref[...], gn_ref[...], sh_ref[...], sc_ref[...]).astype(bf16)
    z = jnp.dot(a, w_ref[...], preferred_element_type=f32)
    d = bg_ref.shape[-1]
    bg_ref[...] = z[:, :d].astype(bf16)
    cu_ref[...] = (z[:, d:2 * d] * z[:, 2 * d:]).astype(bf16)


def _lane_groups(x, op):
    parts = [x[:, i * LANES:(i + 1) * LANES] for i in range(x.shape[1] // LANES)]
    while len(parts) > 1:
        parts = [op(parts[i], parts[i + 1]) if i + 1 < len(parts) else parts[i]
                 for i in range(0, len(parts), 2)]
    return parts[0]


def _attn_body(q_ref, k_ref, v_ref, o_ref, s_scr, *, n_ctx, n_all, heads_per_step, group, dk, dv):
    def run(nk):
        chunks = [(a, min(a + ATTN_KEY_CHUNK, nk)) for a in range(0, nk, ATTN_KEY_CHUNK)]
        for j in range(heads_per_step):
            kv = j // group
            q = q_ref[:, j * dk:(j + 1) * dk]
            m = None
            for a, b in chunks:
                s = lax.dot_general(q, k_ref[a:b, kv * dk:(kv + 1) * dk], (((1,), (1,)), ((), ())),
                                    preferred_element_type=f32)
                s_scr[j, :, a:b] = s
                mc = _lane_groups(s, jnp.maximum)
                m = mc if m is None else jnp.maximum(m, mc)
            m = jnp.max(m, axis=-1, keepdims=True)
            o = l = None
            for a, b in chunks:
                p = jnp.exp2(s_scr[j, :, a:b] - m)
                lc = _lane_groups(p, jnp.add)
                oc = jnp.dot(p.astype(bf16), v_ref[a:b, kv * dv:(kv + 1) * dv], preferred_element_type=f32)
                o, l = (oc, lc) if o is None else (o + oc, l + lc)
            l = jnp.sum(l, axis=-1, keepdims=True)
            o_ref[:, j * dv:(j + 1) * dv] = (o / l).astype(o_ref.dtype)

    i = pl.program_id(2)
    pl.when(i == 0)(lambda: run(n_ctx))
    pl.when(i > 0)(lambda: run(n_all))


def _attn_call(q, k, v, n_batch, heads, kv_heads, dk, dv, n_ctx):
    nt = q.shape[0] // n_batch
    g = heads // kv_heads
    hp = ATTN_HEADS_PER_STEP
    kh = hp // g
    q3, k3, v3 = (a.reshape(n_batch, nt, a.shape[-1]) for a in (q, k, v))
    out = pl.pallas_call(
        functools.partial(_attn_body, n_ctx=n_ctx, n_all=nt, heads_per_step=hp, group=g, dk=dk, dv=dv),
        grid=(n_batch, heads // hp, nt // TM),
        in_specs=[pl.BlockSpec((None, TM, hp * dk), lambda b, h, i: (b, i, h)),
                  pl.BlockSpec((None, nt, kh * dk), lambda b, h, i: (b, 0, h)),
                  pl.BlockSpec((None, nt, kh * dv), lambda b, h, i: (b, 0, h))],
        out_specs=pl.BlockSpec((None, TM, hp * dv), lambda b, h, i: (b, i, h)),
        out_shape=jax.ShapeDtypeStruct((n_batch, nt, heads * dv), bf16),
        scratch_shapes=[pltpu.VMEM((hp, TM, nt), f32)],
        compiler_params=_params("arbitrary", "arbitrary", "arbitrary"),
        name="attention",
    )(q3, k3, v3)
    return out.reshape(n_batch * nt, heads * dv)


def _route_epilogue(x, y, g1_ref, gn2_ref, sh2_ref, sc2_ref, wr_ref, br_ref, ltri_ref, upper_ref,
                    xo_ref, f_ref, meta_ref, wts_ref, runs_ref, cnt_ref, carry_ref, lg_scr):
    i = pl.program_id(0)

    @pl.when(i == 0)
    def _():
        carry_ref[...] = jnp.zeros_like(carry_ref)
        lg_scr[...] = jnp.zeros_like(lg_scr)

    xn = x + g1_ref[...] * y
    xo_ref[...] = xn
    f = _modulate(xn, gn2_ref[...], sh2_ref[...], sc2_ref[...])
    f_hi = f.astype(bf16)
    f_ref[...] = f_hi
    f_lo = (f - f_hi.astype(f32)).astype(bf16)
    logits = lg_scr[(i + 1) % 2]
    lg_scr[i % 2] = (jnp.dot(f_hi, wr_ref[0], preferred_element_type=f32)
                     + jnp.dot(f_lo, wr_ref[0], preferred_element_type=f32)
                     + jnp.dot(f_hi, wr_ref[1], preferred_element_type=f32))
    routed = (i > 0).astype(f32)
    bias = br_ref[...]

    def pick(onehot, v):
        return jnp.sum(jnp.where(onehot, v, 0.0), axis=-1, keepdims=True)

    def route_rows(lg):
        lane = lax.broadcasted_iota(jnp.int32, lg.shape, 1).astype(f32)

        def first_argmax(score):
            top = jnp.max(score, axis=-1, keepdims=True)
            return jnp.min(jnp.where(score == top, lane, float(LANES)), axis=-1, keepdims=True)

        def masked_softmax(mask):
            z = jnp.where(mask, lg, NEG)
            e = jnp.exp(z - jnp.max(z, axis=-1, keepdims=True))
            return e / jnp.sum(e, axis=-1, keepdims=True)

        gmask = lane < float(MOE_GROUPS)
        gp = masked_softmax(gmask)
        g_sel = first_argmax(jnp.where(gmask, gp + bias, NEG))
        g_w = pick(lane == g_sel, gp)
        lo = float(ROUTE_LANE0) + float(MOE_EXPERTS_PER_GROUP) * g_sel
        emask = (lane >= lo) & (lane < lo + float(MOE_EXPERTS_PER_GROUP))
        ep = masked_softmax(emask)
        escore = jnp.where(emask, ep + bias, NEG)
        i1 = first_argmax(escore)
        oh1 = lane == i1
        i2 = first_argmax(jnp.where(oh1, NEG, escore))
        oh2 = lane == i2
        p1, p2 = pick(oh1, ep), pick(oh2, ep)
        return (jnp.where(oh1, 1.0, 0.0), jnp.where(oh2, 1.0, 0.0),
                p1 / (p1 + p2) * g_w, p2 / (p1 + p2) * g_w)

    m1, m2, w1, w2 = route_rows(logits)
    oh1, oh2 = m1 > 0.5, m2 > 0.5
    lane = lax.broadcasted_iota(jnp.int32, logits.shape, 1).astype(f32)

    hits = (m1 + m2) * routed
    before = jnp.dot(ltri_ref[...], hits.astype(bf16), preferred_element_type=f32)
    n_tile = jnp.sum(hits, axis=0, keepdims=True)
    n_al = jnp.floor((n_tile + float(RUN_ALIGN - 1)) * (1.0 / RUN_ALIGN)) * float(RUN_ALIGN)
    lo = jnp.dot(jnp.broadcast_to(n_al, (SUBLANES, LANES)), upper_ref[...], preferred_element_type=f32,
                 precision=lax.Precision.HIGHEST)[0:1, :]
    lp1, lp2 = pick(oh1, before + lo), pick(oh2, before + lo)

    row = lax.broadcasted_iota(jnp.int32, runs_ref.shape, 0)
    runs = jnp.where(row == 0, n_tile, jnp.where(row == 1, carry_ref[...], jnp.where(row == 2, lo, 0.0)))
    runs_ref[...] = runs.astype(jnp.int32)
    carry_ref[...] = carry_ref[...] + n_tile
    cnt_ref[...] = carry_ref[...]

    meta = jnp.where(lane == 0.0, lp1, jnp.where(lane == 1.0, lp2, 0.0))
    meta_ref[...] = meta.T[:meta_ref.shape[0], :].astype(jnp.int32)
    col = lax.broadcasted_iota(jnp.int32, wts_ref.shape, 1)
    wts_ref[...] = jnp.where(col == 0, w1, jnp.where(col == 1, w2, jnp.where(col == 2, lp1, lp2)))


def _attn_out_body(y_ref, wo_ref, x_ref, *rest):
    y = jnp.dot(y_ref[...], wo_ref[...], preferred_element_type=f32)
    _route_epilogue(x_ref[...], y, *rest)


def _conv_out_body(cu_ref, prev_ref, next_ref, bg_ref, cw_ref, wo_ref, x_ref, *rest, tiles_per_b):
    t = jnp.minimum(pl.program_id(0), pl.num_programs(0) - 2) % tiles_per_b
    has_prev = (t >= 2).astype(f32)
    has_next = ((t != 0) & (t != tiles_per_b - 1)).astype(f32)
    cu = cu_ref[...].astype(f32)
    row = lax.broadcasted_iota(jnp.int32, cu.shape, 0)
    hp = prev_ref[HALO - 1:HALO, :].astype(f32) * has_prev
    hn = next_ref[0:1, :].astype(f32) * has_next
    up = jnp.where(row == 0, hp, pltpu.roll(cu, 1, 0))
    dn = jnp.where(row == TM - 1, hn, pltpu.roll(cu, TM - 1, 0))
    cw = cw_ref[...]
    z = cw[0:1, :] * up + cw[1:2, :] * cu + cw[2:3, :] * dn
    yb = (bg_ref[...].astype(f32) * z).astype(bf16)
    y = jnp.dot(yb, wo_ref[...], preferred_element_type=f32)
    _route_epilogue(x_ref[...], y, *rest)


def _mixer_tile_spec(n, last):
    return pl.BlockSpec((TM, n), lambda i: (jnp.minimum(i, last), 0))


def _route_specs(d, tiles_per_b, n_batch):
    last = tiles_per_b * n_batch - 1

    def routed(i):
        return jnp.maximum(i - 1, 0)

    ins = [_mod_spec(2, tiles_per_b, n_batch), _row_spec(d), _mod_spec(3, tiles_per_b, n_batch),
           _mod_spec(4, tiles_per_b, n_batch), _full_spec((2, d, LANES)), _row_spec(LANES),
           _full_spec((TM, TM)), _full_spec((LANES, LANES))]
    outs = [_mixer_tile_spec(d, last), _mixer_tile_spec(d, last),
            pl.BlockSpec((None, META_ROWS, TM), lambda i: (routed(i), 0, 0)),
            pl.BlockSpec((TM, 8), lambda i: (routed(i), 0)),
            pl.BlockSpec((None, SUBLANES, LANES), lambda i: (routed(i), 0, 0)), _row_spec(LANES)]
    return ins, outs


def _route_out_shapes(t, d):
    return [jax.ShapeDtypeStruct((t, d), f32), jax.ShapeDtypeStruct((t, d), bf16),
            jax.ShapeDtypeStruct((t // TM, META_ROWS, TM), jnp.int32), jax.ShapeDtypeStruct((t, 8), f32),
            jax.ShapeDtypeStruct((t // TM, SUBLANES, LANES), jnp.int32),
            jax.ShapeDtypeStruct((1, LANES), f32)]


def _rows_to_tiles(tref, x):
    n = x.shape[0]
    for s in range(SUBLANES):
        tref[pl.ds(s, n, stride=SUBLANES), :] = x[:, s * LANES:(s + 1) * LANES]


def _tiles_to_rows(tref):
    n = tref.shape[0] // SUBLANES
    return jnp.concatenate([tref[pl.ds(s, n, stride=SUBLANES), :] for s in range(SUBLANES)], axis=-1)


def _tile_rows(ref, row, n):
    return ref.at[pl.ds(row * SUBLANES, n * SUBLANES)]


def _run_copies(chunks_ref, copy):
    n_chunks = chunks_ref[1, 0]

    def chunk(q, carry):
        copy(q * RUN_ALIGN, chunks_ref[0, q]).start()
        return carry

    lax.fori_loop(0, n_chunks, chunk, 0)
    return n_chunks


def _wait_chunks(n_chunks, copy):
    def wait_rows(rows):
        def wait_one(c, carry):
            copy(0, 0, rows).wait()
            return carry
        return wait_one

    lax.fori_loop(0, n_chunks // WAIT_GROUP, wait_rows(WAIT_GROUP * RUN_ALIGN), 0)
    lax.fori_loop(0, n_chunks % WAIT_GROUP, wait_rows(RUN_ALIGN), 0)


def _dispatch_body(seg_ref, chunks_ref, meta_ref, f_ref, xp_ref, sbuf, zbuf, pending, sem, zsem):
    i = pl.program_id(0)
    n_blocks = xp_ref.shape[0] // (EXPERT_BLOCK * SUBLANES)

    @pl.when(i == 0)
    def _():
        zbuf[...] = jnp.zeros_like(zbuf)

        def pad_fill(e, wait):
            n_pad = seg_ref[2 * MOE_EXPERTS + e]
            off = seg_ref[MOE_EXPERTS + e]
            for piece in PAD_PIECES:
                take = (n_pad & piece) != 0
                cp = pltpu.make_async_copy(_tile_rows(zbuf, 0, piece), _tile_rows(xp_ref, off, piece), zsem)
                pl.when(take)(cp.wait if wait else cp.start)
                off = off + jnp.where(take, piece, 0)

        def tail_fill(b, wait):
            cp = pltpu.make_async_copy(zbuf, _tile_rows(xp_ref, b * EXPERT_BLOCK, EXPERT_BLOCK), zsem)
            cp.wait() if wait else cp.start()

        for wait in (False, True):
            lax.fori_loop(0, MOE_EXPERTS, lambda e, c, w=wait: (pad_fill(e, w), c)[1], 0)
            lax.fori_loop(seg_ref[3 * MOE_EXPERTS], n_blocks, lambda b, c, w=wait: (tail_fill(b, w), c)[1], 0)

        pending[0] = 0

    buf = sbuf.at[i % 2]
    pos = lax.broadcasted_iota(jnp.int32, (LOCAL_ROWS, TM), 0)
    m = meta_ref[...]
    perm = jnp.where((pos == m[0:1, :]) | (pos == m[1:2, :]), 1.0, 0.0).astype(bf16)
    _rows_to_tiles(buf, jnp.dot(perm, f_ref[...], preferred_element_type=f32))

    def copy(local_row, slot_row, rows=RUN_ALIGN):
        return pltpu.make_async_copy(_tile_rows(buf, local_row, rows), _tile_rows(xp_ref, slot_row, rows), sem)

    _wait_chunks(pending[0], copy)
    pending[0] = _run_copies(chunks_ref, copy)

    @pl.when(i == pl.num_programs(0) - 1)
    def _():
        _wait_chunks(pending[0], copy)


def _dispatch_call(seg, chunks, meta, f, n_rows):
    t, d = f.shape
    grid_spec = pltpu.PrefetchScalarGridSpec(
        num_scalar_prefetch=1,
        grid=(t // TM,),
        in_specs=[pl.BlockSpec((None, SUBLANES, LANES), lambda i, seg: (i, 0, 0), memory_space=pltpu.SMEM),
                  pl.BlockSpec((None, META_ROWS, TM), lambda i, seg: (i, 0, 0)),
                  pl.BlockSpec((TM, d), lambda i, seg: (i, 0))],
        out_specs=pl.BlockSpec(memory_space=pl.ANY),
        scratch_shapes=[pltpu.VMEM((2, LOCAL_ROWS * SUBLANES, LANES), f32),
                        pltpu.VMEM((EXPERT_BLOCK * SUBLANES, LANES), f32),
                        pltpu.SMEM((1,), jnp.int32),
                        pltpu.SemaphoreType.DMA(()), pltpu.SemaphoreType.DMA(())],
    )
    return pl.pallas_call(
        _dispatch_body,
        grid_spec=grid_spec,
        out_shape=jax.ShapeDtypeStruct((n_rows * SUBLANES, LANES), f32),
        compiler_params=_params("arbitrary"),
        name="moe_dispatch",
    )(seg, chunks, meta, f)


def _expert_body(be_ref, na_ref, x_ref, wg_hbm, wu_hbm, wd_hbm, o_ref,
                 wg_f, wu_f, wd_f, wg_s, wu_s, wd_s, wsem, *, layer):
    i = pl.program_id(0)
    active = i < na_ref[0]
    e = be_ref[i]
    changed = (i == 0) | (e != be_ref[jnp.maximum(i - 1, 0)])

    def fetch(expert):
        pairs = ((wg_hbm, wg_f), (wu_hbm, wu_f), (wd_hbm, wd_f))
        return [pltpu.make_async_copy(w.at[layer, expert], buf, wsem.at[j]) for j, (w, buf) in enumerate(pairs)]

    @pl.when(i == 0)
    def _():
        for cp in fetch(e):
            cp.start()

    @pl.when(active & changed)
    def _():
        for cp in fetch(e):
            cp.wait()
        wg_s[...] = wg_f[...].astype(bf16)
        wu_s[...] = wu_f[...].astype(bf16)
        wd_s[...] = wd_f[...].astype(bf16)

        @pl.when(e + 1 < MOE_EXPERTS)
        def _():
            for cp in fetch(e + 1):
                cp.start()

    @pl.when(active)
    def _():
        x = _tiles_to_rows(x_ref).astype(bf16)
        g = jnp.dot(x, wg_s[...], preferred_element_type=f32)
        u = jnp.dot(x, wu_s[...], preferred_element_type=f32)
        h = (_silu(g) * u).astype(bf16)
        _rows_to_tiles(o_ref, jnp.dot(h, wd_s[...], preferred_element_type=f32))

    @pl.when(jnp.logical_not(active))
    def _():
        o_ref[...] = jnp.zeros_like(o_ref)


def _expert_call(blk_e, n_act, xp, w_gate, w_up, w_down, layer):
    n_rows = xp.shape[0] // SUBLANES
    d, ff = w_gate.shape[-2:]
    blk_rows = EXPERT_BLOCK * SUBLANES

    def blk(i, be, na):
        return jnp.maximum(jnp.minimum(i, na[0] - 1), 0)

    any_spec = pl.BlockSpec(memory_space=pl.ANY)
    grid_spec = pltpu.PrefetchScalarGridSpec(
        num_scalar_prefetch=2,
        grid=(n_rows // EXPERT_BLOCK,),
        in_specs=[pl.BlockSpec((blk_rows, LANES), lambda i, be, na: (blk(i, be, na), 0)),
                  any_spec, any_spec, any_spec],
        out_specs=pl.BlockSpec((blk_rows, LANES), lambda i, be, na: (i, 0)),
        scratch_shapes=[pltpu.VMEM((d, ff), f32), pltpu.VMEM((d, ff), f32), pltpu.VMEM((ff, d), f32),
                        pltpu.VMEM((d, ff), bf16), pltpu.VMEM((d, ff), bf16), pltpu.VMEM((ff, d), bf16),
                        pltpu.SemaphoreType.DMA((3,))],
    )
    return pl.pallas_call(
        functools.partial(_expert_body, layer=layer),
        grid_spec=grid_spec,
        out_shape=jax.ShapeDtypeStruct((n_rows * SUBLANES, LANES), f32),
        compiler_params=_params("arbitrary"),
        name="moe_experts",
    )(blk_e, n_act, xp, w_gate, w_up, w_down)


def _combine_body(chunks_ref, next_chunks_ref, x_ref, wts_ref, g2_ref, fg_ref, yp_ref, o_ref,
                  gbuf, pending, sem, *, final):
    step = pl.program_id(0) * pl.num_programs(1) + pl.program_id(1)
    n_steps = pl.num_programs(0) * pl.num_programs(1)
    cur = step % 2

    def fetch(buf):
        def copy(local_row, slot_row, rows=RUN_ALIGN):
            return pltpu.make_async_copy(_tile_rows(yp_ref, slot_row, rows),
                                         _tile_rows(gbuf.at[buf], local_row, rows), sem.at[buf])
        return copy

    @pl.when(step == 0)
    def _():
        gbuf[...] = jnp.zeros_like(gbuf)
        pending[0] = _run_copies(chunks_ref, fetch(0))

    @pl.when(step + 1 < n_steps)
    def _():
        pending[1 - cur] = _run_copies(next_chunks_ref, fetch(1 - cur))

    _wait_chunks(pending[cur], fetch(cur))
    ys = _tiles_to_rows(gbuf.at[cur]).astype(bf16)
    w = wts_ref[...]
    pos = lax.broadcasted_iota(jnp.int32, (TM, LOCAL_ROWS), 1).astype(f32)
    y = None
    for k in range(MOE_TOP_K):
        pick = jnp.where(pos == w[:, MOE_TOP_K + k:MOE_TOP_K + k + 1], 1.0, 0.0).astype(bf16)
        yk = w[:, k:k + 1] * jnp.dot(pick, ys, preferred_element_type=f32)
        y = yk if y is None else y + yk
    xn = x_ref[...] + g2_ref[...] * y
    if final:
        xn = _rms(xn) * fg_ref[...]
    o_ref[...] = xn


def _combine_call(chunks, x, wts, mods3, final_g, yp, n_batch, tiles_per_b, final):
    t, d = x.shape
    t0 = 1 if final else 0
    nt = tiles_per_b - t0

    def tile(b, j):
        return b * tiles_per_b + t0 + j

    def next_tile(b, j):
        nxt = jnp.minimum(b * nt + j + 1, n_batch * nt - 1)
        return tile(nxt // nt, nxt % nt)

    def g2_map(b, j):
        r = jnp.where(t0 + j == 0, n_batch, b)
        return (r * N_MOD + 5, 0, 0)

    return pl.pallas_call(
        functools.partial(_combine_body, final=final),
        grid=(n_batch, nt),
        in_specs=[pl.BlockSpec((None, SUBLANES, LANES), lambda b, j: (tile(b, j), 0, 0),
                               memory_space=pltpu.SMEM),
                  pl.BlockSpec((None, SUBLANES, LANES), lambda b, j: (next_tile(b, j), 0, 0),
                               memory_space=pltpu.SMEM),
                  pl.BlockSpec((TM, d), lambda b, j: (tile(b, j), 0)),
                  pl.BlockSpec((TM, 8), lambda b, j: (tile(b, j), 0)),
                  pl.BlockSpec((None, 1, d), g2_map),
                  pl.BlockSpec((1, d), lambda b, j: (0, 0)),
                  pl.BlockSpec(memory_space=pl.ANY)],
        out_specs=pl.BlockSpec((TM, d), lambda b, j: (b * nt + j, 0)),
        scratch_shapes=[pltpu.VMEM((2, LOCAL_ROWS * SUBLANES, LANES), f32), pltpu.SMEM((2,), jnp.int32),
                        pltpu.SemaphoreType.DMA((2,))],
        out_shape=jax.ShapeDtypeStruct((n_batch * nt * TM, d), f32),
        compiler_params=_params("arbitrary", "arbitrary"),
        name="moe_combine",
    )(chunks, chunks, x, wts, mods3, final_g, yp)


def _chunk_table(runs, pstarts):
    n_chunk_lanes = LOCAL_ROWS // RUN_ALIGN
    assert n_chunk_lanes <= LANES
    experts = slice(ROUTE_LANE0, ROUTE_LANE0 + MOE_EXPERTS)
    n, base, lo = runs[:, 0, experts], runs[:, 1, experts], runs[:, 2, experts]
    ends = lo + (n + RUN_ALIGN - 1) // RUN_ALIGN * RUN_ALIGN
    row0 = jnp.arange(LANES, dtype=jnp.int32) * RUN_ALIGN
    owner = jnp.sum((ends[:, None, :] <= row0[None, :, None]).astype(jnp.int32), axis=-1)
    shift = pstarts[None, :] + base - lo
    mine = owner[:, :, None] == jnp.arange(MOE_EXPERTS, dtype=jnp.int32)[None, None, :]
    slot = row0[None, :] + jnp.sum(jnp.where(mine, shift[:, None, :], 0), axis=-1)
    count = jnp.broadcast_to(ends[:, -1:] // RUN_ALIGN, slot.shape)
    rest = jnp.zeros((runs.shape[0], SUBLANES - 2, LANES), jnp.int32)
    return jnp.concatenate([slot[:, None, :], count[:, None, :], rest], axis=1).astype(jnp.int32)
def _rope_tables(seq, n_ctx, half):
    inv = ROPE_THETA ** (-jnp.arange(half, dtype=f32) / half)
    t = jnp.arange(seq, dtype=jnp.int32)
    ang_r = (t // GRID_W).astype(f32)[:, None] * inv[None, :]
    ang_c = (t % GRID_W).astype(f32)[:, None] * inv[None, :]
    gap = jnp.zeros((seq, LANES // 2 - 2 * half), f32)
    cos_h = jnp.concatenate([jnp.cos(ang_r), jnp.cos(ang_c), gap], axis=-1)
    sin_h = jnp.concatenate([jnp.sin(ang_r), jnp.sin(ang_c), gap], axis=-1)
    cos = jnp.concatenate([cos_h, cos_h], axis=-1)
    sin = jnp.concatenate([-sin_h, sin_h], axis=-1)
    ctx_c = jnp.broadcast_to((cos[:1] != 0.0).astype(f32), (n_ctx, LANES))
    ctx_s = jnp.zeros((n_ctx, LANES), f32)
    return jnp.concatenate([ctx_c, cos], axis=0), jnp.concatenate([ctx_s, sin], axis=0)


def _rope_lanes(w, half):
    lead = w.shape[:-1]
    halves = w.reshape(lead + (2, 2, half)).swapaxes(-3, -2).reshape(lead + (2, 2 * half))
    pad = [(0, 0)] * (len(lead) + 1) + [(0, LANES // 2 - 2 * half)]
    return jnp.pad(halves, pad).reshape(lead + (LANES,))


def _table_spec(tiles_per_b):
    return pl.BlockSpec((TM, LANES), lambda i: (i % tiles_per_b, 0))


def kernel(x, c, ctx, c_ctx, w_mod, b_mod, norm_mix_g, norm_ffn_g, mla_w_dq, mla_g_q, mla_w_uq, mla_w_dkv, mla_g_kv, mla_w_ukv, mla_w_o, conv_w_in, conv_w, conv_w_out, gqa_w_qkv, gqa_q_norm_g, gqa_k_norm_g, gqa_w_o, moe_w_group, moe_b_group, moe_w_expert, moe_b_expert, moe_w_gate, moe_w_up, moe_w_down, final_norm_g):
    n_batch, seq, d = x.shape
    n_ctx = ctx.shape[1]
    depth = w_mod.shape[0]
    assert n_ctx == TM and seq % TM == 0 and d == SUBLANES * LANES
    nt = n_ctx + seq
    tiles_per_b = nt // TM
    t_all = n_batch * nt
    n_tiles = t_all // TM
    n_assign = t_all * MOE_TOP_K
    n_blocks = -(-(n_assign + MOE_EXPERTS * (RUN_ALIGN - 1)) // EXPERT_BLOCK) + MOE_EXPERTS
    n_rows = n_blocks * EXPERT_BLOCK

    xs = jnp.concatenate([ctx, x], axis=1).reshape(t_all, d)
    cond8 = jnp.concatenate([c, c_ctx[None], jnp.zeros((8 - n_batch - 1, d), f32)], axis=0)
    mods = _mod_call(cond8, w_mod, b_mod)
    mods = mods.reshape(depth, 8 * N_MOD, 1, d)

    mla_tabs = _rope_tables(seq, n_ctx, MLA_ROPE // 4)
    gqa_tabs = _rope_tables(seq, n_ctx, GQA_HEAD_DIM // 4)
    ltri = jnp.tril(jnp.ones((TM, TM), f32), -1).astype(bf16)
    upper = jnp.triu(jnp.ones((LANES, LANES), f32), 1)
    mspec = functools.partial(_mod_spec, tiles_per_b=tiles_per_b, n_batch=n_batch)
    tab_specs = [_table_spec(tiles_per_b)] * 2
    route_in, route_out = _route_specs(d, tiles_per_b, n_batch)
    route_shapes = _route_out_shapes(t_all, d)
    route_scratch = [pltpu.VMEM((1, LANES), f32), pltpu.VMEM((2, TM, LANES), f32)]
    mtile = functools.partial(_mixer_tile_spec, last=n_tiles - 1)

    for i in range(depth):
        kind, j = i % N_MIXERS, i // N_MIXERS
        m3 = mods[i]
        gmix = norm_mix_g[i][None]
        head_in = [_tile_spec(d), mspec(0), mspec(1), _row_spec(d)]
        wr = jnp.pad(jnp.concatenate([moe_w_group[i], moe_w_expert[i]], axis=1),
                     ((0, 0), (0, LANES - MOE_GROUPS - MOE_EXPERTS)))
        wr_hi = wr.astype(bf16)
        wr = jnp.stack([wr_hi, (wr - wr_hi.astype(f32)).astype(bf16)])
        br = jnp.pad(jnp.concatenate([moe_b_group[i], moe_b_expert[i]]),
                     (0, LANES - MOE_GROUPS - MOE_EXPERTS))[None]
        route_args = (m3, norm_ffn_g[i][None], m3, m3, wr, br, ltri, upper)

        if kind == 0:
            dk, dv = 2 * LANES, MLA_V
            half = MLA_ROPE // 4
            wcat = jnp.concatenate([mla_w_dq[j], mla_w_dkv[j][:, :MLA_KV_LORA],
                                    _rope_lanes(mla_w_dkv[j][:, MLA_KV_LORA:], half)], axis=1).astype(bf16)
            wuq3 = mla_w_uq[j].reshape(MLA_Q_LORA, MLA_HEADS, MLA_NOPE + MLA_ROPE)
            wuq = jnp.concatenate([wuq3[..., :MLA_NOPE], _rope_lanes(wuq3[..., MLA_NOPE:], half)], axis=-1
                                  ).reshape(MLA_Q_LORA, MLA_HEADS * dk).astype(bf16)
            wukv3 = mla_w_ukv[j].reshape(MLA_KV_LORA, MLA_HEADS, MLA_NOPE + MLA_V)
            wukv = jnp.concatenate([wukv3[..., :MLA_NOPE].reshape(MLA_KV_LORA, -1),
                                    wukv3[..., MLA_NOPE:].reshape(MLA_KV_LORA, -1)], axis=1).astype(bf16)
            q, k, v = pl.pallas_call(
                functools.partial(_mla_proj_body, scale=(MLA_NOPE + MLA_ROPE) ** -0.5 * LOG2E),
                grid=(n_tiles,),
                in_specs=head_in + [_full_spec(wcat.shape), _row_spec(MLA_Q_LORA), _row_spec(MLA_KV_LORA),
                                    _full_spec(wuq.shape), _full_spec(wukv.shape)] + tab_specs,
                out_specs=[_tile_spec(MLA_HEADS * dk), _tile_spec(MLA_HEADS * dk), _tile_spec(MLA_HEADS * dv)],
                out_shape=[jax.ShapeDtypeStruct((t_all, MLA_HEADS * dk), bf16),
                           jax.ShapeDtypeStruct((t_all, MLA_HEADS * dk), bf16),
                           jax.ShapeDtypeStruct((t_all, MLA_HEADS * dv), bf16)],
                compiler_params=_params("arbitrary"),
                name="mla_proj",
            )(xs, m3, m3, gmix, wcat, mla_g_q[j][None], mla_g_kv[j][None], wuq, wukv, *mla_tabs)
            att = _attn_call(q, k, v, n_batch, MLA_HEADS, MLA_HEADS, dk, dv, n_ctx)
            w_o = mla_w_o[j].astype(bf16)
        elif kind == 2:
            dk = dv = GQA_HEAD_DIM
            half = GQA_HEAD_DIM // 4
            n_qk = GQA_HEADS + GQA_KV_HEADS
            wqk = _rope_lanes(gqa_w_qkv[j][:, :n_qk * dk].reshape(d, n_qk, dk), half).reshape(d, n_qk * dk)
            wqkv = jnp.concatenate([wqk, gqa_w_qkv[j][:, n_qk * dk:]], axis=1).astype(bf16)
            q, k, v = pl.pallas_call(
                functools.partial(_gqa_proj_body, scale=GQA_HEAD_DIM ** -0.5 * LOG2E),
                grid=(n_tiles,),
                in_specs=head_in + [_full_spec(wqkv.shape), _row_spec(dk), _row_spec(dk)] + tab_specs,
                out_specs=[_tile_spec(GQA_HEADS * dk), _tile_spec(GQA_KV_HEADS * dk),
                           _tile_spec(GQA_KV_HEADS * dv)],
                out_shape=[jax.ShapeDtypeStruct((t_all, GQA_HEADS * dk), bf16),
                           jax.ShapeDtypeStruct((t_all, GQA_KV_HEADS * dk), bf16),
                           jax.ShapeDtypeStruct((t_all, GQA_KV_HEADS * dv), bf16)],
                compiler_params=_params("arbitrary"),
                name="gqa_proj",
            )(xs, m3, m3, gmix, wqkv, _rope_lanes(gqa_q_norm_g[j][None], half),
              _rope_lanes(gqa_k_norm_g[j][None], half), *gqa_tabs)
            att = _attn_call(q, k, v, n_batch, GQA_HEADS, GQA_KV_HEADS, dk, dv, n_ctx)
            w_o = gqa_w_o[j].astype(bf16)

        if kind == 1:
            w_in = conv_w_in[j].astype(bf16)
            bg, cu = pl.pallas_call(
                _conv_proj_body,
                grid=(n_tiles,),
                in_specs=head_in + [_full_spec(w_in.shape)],
                out_specs=[_tile_spec(d), _tile_spec(d)],
                out_shape=[jax.ShapeDtypeStruct((t_all, d), bf16)] * 2,
                compiler_params=_params("arbitrary"),
                name="conv_proj",
            )(xs, m3, m3, gmix, w_in)
            w_o = conv_w_out[j].astype(bf16)
            per = TM // HALO
            last = t_all // HALO - 1
            xs, f, meta, wts, runs, cnt = pl.pallas_call(
                functools.partial(_conv_out_body, tiles_per_b=tiles_per_b),
                grid=(n_tiles + 1,),
                in_specs=[mtile(d),
                          pl.BlockSpec((HALO, d), lambda i: (
                              jnp.maximum(jnp.minimum(i, n_tiles - 1) * per - 1, 0), 0)),
                          pl.BlockSpec((HALO, d), lambda i: (jnp.minimum((i + 1) * per, last), 0)),
                          mtile(d), _full_spec((3, d)), _full_spec(w_o.shape), mtile(d)] + route_in,
                out_specs=route_out,
                out_shape=route_shapes,
                scratch_shapes=route_scratch,
                compiler_params=_params("arbitrary"),
                name="conv_out_route",
            )(cu, cu, cu, bg, conv_w[j], w_o, xs, *route_args)
        else:
            xs, f, meta, wts, runs, cnt = pl.pallas_call(
                _attn_out_body,
                grid=(n_tiles + 1,),
                in_specs=[mtile(att.shape[-1]), _full_spec(w_o.shape), mtile(d)] + route_in,
                out_specs=route_out,
                out_shape=route_shapes,
                scratch_shapes=route_scratch,
                compiler_params=_params("arbitrary"),
                name="attn_out_route",
            )(att, w_o, xs, *route_args)

        counts = cnt[0, ROUTE_LANE0:ROUTE_LANE0 + MOE_EXPERTS].astype(jnp.int32)
        padded = (counts + RUN_ALIGN - 1 + EXPERT_BLOCK - 1) // EXPERT_BLOCK * EXPERT_BLOCK
        pends = jnp.cumsum(padded)
        pstarts = pends - padded
        blk_row0 = jnp.arange(n_blocks, dtype=jnp.int32) * EXPERT_BLOCK
        blk_e = jnp.minimum(jnp.sum((pends[None, :] <= blk_row0[:, None]).astype(jnp.int32), axis=1),
                            MOE_EXPERTS - 1)
        n_act = (pends[-1:] // EXPERT_BLOCK).astype(jnp.int32)
        seg = jnp.concatenate([pstarts, pstarts + counts, padded - counts, n_act,
                               jnp.zeros((MOE_EXPERTS - 1,), jnp.int32)]).astype(jnp.int32)

        chunks = _chunk_table(runs, pstarts)

        xp = _dispatch_call(seg, chunks, meta, f, n_rows)
        yp = _expert_call(blk_e, n_act, xp, moe_w_gate, moe_w_up, moe_w_down, i)
        final = i == depth - 1
        xs = _combine_call(chunks, xs, wts, m3, final_norm_g[None], yp, n_batch, tiles_per_b, final)

    return xs.reshape(n_batch, seq, d)
```

```python
import functools

import jax
import jax.numpy as jnp
from jax import lax
from jax.experimental import pallas as pl
from jax.experimental.pallas import tpu as pltpu

f32 = jnp.float32
bf16 = jnp.bfloat16

GRID_W = 64
ROPE_THETA = 10000.0
NORM_EPS = 1e-6
N_MOD = 6
N_MIXERS = 3
MLA_HEADS, MLA_Q_LORA, MLA_KV_LORA, MLA_NOPE, MLA_ROPE, MLA_V = 8, 512, 256, 128, 64, 128
GQA_HEADS, GQA_KV_HEADS, GQA_HEAD_DIM = 16, 8, 128
MOE_GROUPS, MOE_EXPERTS_PER_GROUP, MOE_TOP_K = 4, 8, 2
MOE_EXPERTS = MOE_GROUPS * MOE_EXPERTS_PER_GROUP

LANES = 128
TM = 256
EXPERT_BLOCK = 256
SUBLANES = 8
RUN_ALIGN = 8
WAIT_GROUP = 8
PAD_PIECES = tuple(1 << b for b in reversed(range(EXPERT_BLOCK.bit_length())))
HALO = 16
ATTN_KEY_CHUNK = 256
ATTN_HEADS_PER_STEP = 4
LOG2E = 1.4426950408889634
VMEM_LIMIT = 56 * 1024 * 1024
ROUTE_LANE0 = MOE_GROUPS
META_ROWS = 8
MXU_TILE = 256
LOCAL_ROWS = -(-(TM * MOE_TOP_K + MOE_EXPERTS * (RUN_ALIGN - 1)) // MXU_TILE) * MXU_TILE
NEG = -1e30


def _params(*sem):
    return pltpu.CompilerParams(dimension_semantics=sem, vmem_limit_bytes=VMEM_LIMIT)


def _rms(x):
    return x * lax.rsqrt(jnp.mean(x * x, axis=-1, keepdims=True) + NORM_EPS)


def _modulate(x, gn, sh, sc):
    return _rms(x) * gn * (1.0 + sc) + sh


def _silu(x):
    return x / (1.0 + jnp.exp(-x))


def _rope(x, c, s):
    return x * c + pltpu.roll(x, LANES // 2, 1) * s


def _mod_body(cond_ref, w_ref, b_ref, o_ref):
    a = _silu(cond_ref[...])
    o_ref[...] = jnp.dot(a, w_ref[...], preferred_element_type=f32,
                         precision=lax.Precision.HIGHEST) + b_ref[...]


def _mod_call(cond8, w_mod, b_mod):
    depth, d, n6 = w_mod.shape
    tn = n6 // 4
    return pl.pallas_call(
        _mod_body,
        grid=(depth, n6 // tn),
        in_specs=[pl.BlockSpec((8, d), lambda l, j: (0, 0)),
                  pl.BlockSpec((None, d, tn), lambda l, j: (l, 0, j)),
                  pl.BlockSpec((None, 1, tn), lambda l, j: (l, 0, j))],
        out_specs=pl.BlockSpec((None, 8, tn), lambda l, j: (l, 0, j)),
        out_shape=jax.ShapeDtypeStruct((depth, 8, n6), f32),
        compiler_params=_params("arbitrary", "arbitrary"),
        name="mod_vectors",
    )(cond8, w_mod, b_mod.reshape(depth, 1, n6))


def _mod_spec(chunk, tiles_per_b, n_batch, d=SUBLANES * LANES):
    def imap(i):
        i = jnp.minimum(i, tiles_per_b * n_batch - 1)
        r = jnp.where(i % tiles_per_b == 0, n_batch, i // tiles_per_b)
        return (r * N_MOD + chunk, 0, 0)
    return pl.BlockSpec((None, 1, d), imap)


def _row_spec(n):
    return pl.BlockSpec((1, n), lambda i: (0, 0))


def _full_spec(shape):
    return pl.BlockSpec(shape, lambda i: (0,) * len(shape))


def _tile_spec(n):
    return pl.BlockSpec((TM, n), lambda i: (i, 0))


def _mla_proj_body(x_ref, sh_ref, sc_ref, gn_ref, wcat_ref, gq_ref, gkv_ref, wuq_ref, wukv_ref,
                   c_ref, s_ref, q_ref, k_ref, v_ref, *, scale):
    a = _modulate(x_ref[...], gn_ref[...], sh_ref[...], sc_ref[...]).astype(bf16)
    z = jnp.dot(a, wcat_ref[...], preferred_element_type=f32)
    cq = (_rms(z[:, :MLA_Q_LORA]) * gq_ref[...]).astype(bf16)
    ckv = (_rms(z[:, MLA_Q_LORA:MLA_Q_LORA + MLA_KV_LORA]) * gkv_ref[...]).astype(bf16)
    kpe = z[:, MLA_Q_LORA + MLA_KV_LORA:]
    q = jnp.dot(cq, wuq_ref[...], preferred_element_type=f32)
    kv = jnp.dot(ckv, wukv_ref[...], preferred_element_type=f32)
    c, s = c_ref[...], s_ref[...]
    kpe_r = _rope(kpe, c, s).astype(bf16)
    hd = 2 * LANES
    for h in range(MLA_HEADS):
        q_ref[:, h * hd:h * hd + LANES] = (q[:, h * hd:h * hd + LANES] * scale).astype(bf16)
        q_ref[:, h * hd + LANES:(h + 1) * hd] = (
            _rope(q[:, h * hd + LANES:(h + 1) * hd], c, s) * scale).astype(bf16)
        k_ref[:, h * hd:h * hd + LANES] = kv[:, h * LANES:(h + 1) * LANES].astype(bf16)
        k_ref[:, h * hd + LANES:(h + 1) * hd] = kpe_r
    v_ref[...] = kv[:, MLA_HEADS * MLA_NOPE:].astype(bf16)


def _gqa_proj_body(x_ref, sh_ref, sc_ref, gn_ref, w_ref, qg_ref, kg_ref,
                   c_ref, s_ref, q_ref, k_ref, v_ref, *, scale):
    a = _modulate(x_ref[...], gn_ref[...], sh_ref[...], sc_ref[...]).astype(bf16)
    z = jnp.dot(a, w_ref[...], preferred_element_type=f32)
    c, s = c_ref[...], s_ref[...]
    dq = GQA_HEADS * GQA_HEAD_DIM
    dkv = GQA_KV_HEADS * GQA_HEAD_DIM
    qg = qg_ref[...] * scale
    kg = kg_ref[...]
    for h in range(GQA_HEADS):
        qh = _rms(z[:, h * LANES:(h + 1) * LANES]) * qg
        q_ref[:, h * LANES:(h + 1) * LANES] = _rope(qh, c, s).astype(bf16)
    for h in range(GQA_KV_HEADS):
        kh = _rms(z[:, dq + h * LANES:dq + (h + 1) * LANES]) * kg
        k_ref[:, h * LANES:(h + 1) * LANES] = _rope(kh, c, s).astype(bf16)
    v_ref[...] = z[:, dq + dkv:].astype(bf16)


def _conv_proj_body(x_ref, sh_ref, sc_ref, gn_ref, w_ref, bg_ref, cu_ref):
    a = _modulate(x_ref[...], gn_ref[...], sh_ref[...], sc_ref[...]).astype(bf16)
    z = jnp.dot(a, w_ref[...], preferred_element_type=f32)
    d = bg_ref.shape[-1]
    bg_ref[...] = z[:, :d].astype(bf16)
    cu_ref[...] = (z[:, d:2 * d] * z[:, 2 * d:]).astype(bf16)


def _lane_groups(x, op):
    parts = [x[:, i * LANES:(i + 1) * LANES] for i in range(x.shape[1] // LANES)]
    while len(parts) > 1:
        parts = [op(parts[i], parts[i + 1]) if i + 1 < len(parts) else parts[i]
                 for i in range(0, len(parts), 2)]
    return parts[0]


def _attn_body(q_ref, k_ref, v_ref, o_ref, s_scr, *, n_ctx, n_all, heads_per_step, group, dk, dv):
    def run(nk):
        chunks = [(a, min(a + ATTN_KEY_CHUNK, nk)) for a in range(0, nk, ATTN_KEY_CHUNK)]
        for j in range(heads_per_step):
            kv = j // group
            q = q_ref[:, j * dk:(j + 1) * dk]
            m = None
            for a, b in chunks:
                s = lax.dot_general(q, k_ref[a:b, kv * dk:(kv + 1) * dk], (((1,), (1,)), ((), ())),
                                    preferred_element_type=f32)
                s_scr[j, :, a:b] = s
                mc = _lane_groups(s, jnp.maximum)
                m = mc if m is None else jnp.maximum(m, mc)
            m = jnp.max(m, axis=-1, keepdims=True)
            o = l = None
            for a, b in chunks:
                p = jnp.exp2(s_scr[j, :, a:b] - m)
                lc = _lane_groups(p, jnp.add)
                oc = jnp.dot(p.astype(bf16), v_ref[a:b, kv * dv:(kv + 1) * dv], preferred_element_type=f32)
                o, l = (oc, lc) if o is None else (o + oc, l + lc)
            l = jnp.sum(l, axis=-1, keepdims=True)
            o_ref[:, j * dv:(j + 1) * dv] = (o / l).astype(o_ref.dtype)

    i = pl.program_id(2)
    pl.when(i == 0)(lambda: run(n_ctx))
    pl.when(i > 0)(lambda: run(n_all))


def _attn_call(q, k, v, n_batch, heads, kv_heads, dk, dv, n_ctx):
    nt = q.shape[0] // n_batch
    g = heads // kv_heads
    hp = ATTN_HEADS_PER_STEP
    kh = hp // g
    q3, k3, v3 = (a.reshape(n_batch, nt, a.shape[-1]) for a in (q, k, v))
    out = pl.pallas_call(
        functools.partial(_attn_body, n_ctx=n_ctx, n_all=nt, heads_per_step=hp, group=g, dk=dk, dv=dv),
        grid=(n_batch, heads // hp, nt // TM),
        in_specs=[pl.BlockSpec((None, TM, hp * dk), lambda b, h, i: (b, i, h)),
                  pl.BlockSpec((None, nt, kh * dk), lambda b, h, i: (b, 0, h)),
                  pl.BlockSpec((None, nt, kh * dv), lambda b, h, i: (b, 0, h))],
        out_specs=pl.BlockSpec((None, TM, hp * dv), lambda b, h, i: (b, i, h)),
        out_shape=jax.ShapeDtypeStruct((n_batch, nt, heads * dv), bf16),
        scratch_shapes=[pltpu.VMEM((hp, TM, nt), f32)],
        compiler_params=_params("arbitrary", "arbitrary", "arbitrary"),
        name="attention",
    )(q3, k3, v3)
    return out.reshape(n_batch * nt, heads * dv)


def _route_epilogue(x, y, g1_ref, gn2_ref, sh2_ref, sc2_ref, wr_ref, br_ref, ltri_ref, upper_ref,
                    xo_ref, f_ref, meta_ref, wts_ref, runs_ref, cnt_ref, carry_ref, lg_scr):
    i = pl.program_id(0)

    @pl.when(i == 0)
    def _():
        carry_ref[...] = jnp.zeros_like(carry_ref)
        lg_scr[...] = jnp.zeros_like(lg_scr)

    xn = x + g1_ref[...] * y
    xo_ref[...] = xn
    f = _modulate(xn, gn2_ref[...], sh2_ref[...], sc2_ref[...])
    f_hi = f.astype(bf16)
    f_ref[...] = f_hi
    f_lo = (f - f_hi.astype(f32)).astype(bf16)
    logits = lg_scr[(i + 1) % 2]
    lg_scr[i % 2] = (jnp.dot(f_hi, wr_ref[0], preferred_element_type=f32)
                     + jnp.dot(f_lo, wr_ref[0], preferred_element_type=f32)
                     + jnp.dot(f_hi, wr_ref[1], preferred_element_type=f32))
    routed = (i > 0).astype(f32)
    bias = br_ref[...]

    def pick(onehot, v):
        return jnp.sum(jnp.where(onehot, v, 0.0), axis=-1, keepdims=True)

    def route_rows(lg):
        lane = lax.broadcasted_iota(jnp.int32, lg.shape, 1).astype(f32)

        def first_argmax(score):
            top = jnp.max(score, axis=-1, keepdims=True)
            return jnp.min(jnp.where(score == top, lane, float(LANES)), axis=-1, keepdims=True)

        def masked_softmax(mask):
            z = jnp.where(mask, lg, NEG)
            e = jnp.exp(z - jnp.max(z, axis=-1, keepdims=True))
            return e / jnp.sum(e, axis=-1, keepdims=True)

        gmask = lane < float(MOE_GROUPS)
        gp = masked_softmax(gmask)
        g_sel = first_argmax(jnp.where(gmask, gp + bias, NEG))
        g_w = pick(lane == g_sel, gp)
        lo = float(ROUTE_LANE0) + float(MOE_EXPERTS_PER_GROUP) * g_sel
        emask = (lane >= lo) & (lane < lo + float(MOE_EXPERTS_PER_GROUP))
        ep = masked_softmax(emask)
        escore = jnp.where(emask, ep + bias, NEG)
        i1 = first_argmax(escore)
        oh1 = lane == i1
        i2 = first_argmax(jnp.where(oh1, NEG, escore))
        oh2 = lane == i2
        p1, p2 = pick(oh1, ep), pick(oh2, ep)
        return (jnp.where(oh1, 1.0, 0.0), jnp.where(oh2, 1.0, 0.0),
                p1 / (p1 + p2) * g_w, p2 / (p1 + p2) * g_w)

    m1, m2, w1, w2 = route_rows(logits)
    oh1, oh2 = m1 > 0.5, m2 > 0.5
    lane = lax.broadcasted_iota(jnp.int32, logits.shape, 1).astype(f32)

    hits = (m1 + m2) * routed
    before = jnp.dot(ltri_ref[...], hits.astype(bf16), preferred_element_type=f32)
    n_tile = jnp.sum(hits, axis=0, keepdims=True)
    n_al = jnp.floor((n_tile + float(RUN_ALIGN - 1)) * (1.0 / RUN_ALIGN)) * float(RUN_ALIGN)
    lo = jnp.dot(jnp.broadcast_to(n_al, (SUBLANES, LANES)), upper_ref[...], preferred_element_type=f32,
                 precision=lax.Precision.HIGHEST)[0:1, :]
    lp1, lp2 = pick(oh1, before + lo), pick(oh2, before + lo)

    row = lax.broadcasted_iota(jnp.int32, runs_ref.shape, 0)
    runs = jnp.where(row == 0, n_tile, jnp.where(row == 1, carry_ref[...], jnp.where(row == 2, lo, 0.0)))
    runs_ref[...] = runs.astype(jnp.int32)
    carry_ref[...] = carry_ref[...] + n_tile
    cnt_ref[...] = carry_ref[...]

    meta = jnp.where(lane == 0.0, lp1, jnp.where(lane == 1.0, lp2, 0.0))
    meta_ref[...] = meta.T[:meta_ref.shape[0], :].astype(jnp.int32)
    col = lax.broadcasted_iota(jnp.int32, wts_ref.shape, 1)
    wts_ref[...] = jnp.where(col == 0, w1, jnp.where(col == 1, w2, jnp.where(col == 2, lp1, lp2)))


def _attn_out_body(y_ref, wo_ref, x_ref, *rest):
    y = jnp.dot(y_ref[...], wo_ref[...], preferred_element_type=f32)
    _route_epilogue(x_ref[...], y, *rest)


def _conv_out_body(cu_ref, prev_ref, next_ref, bg_ref, cw_ref, wo_ref, x_ref, *rest, tiles_per_b):
    t = jnp.minimum(pl.program_id(0), pl.num_programs(0) - 2) % tiles_per_b
    has_prev = (t >= 2).astype(f32)
    has_next = ((t != 0) & (t != tiles_per_b - 1)).astype(f32)
    cu = cu_ref[...].astype(f32)
    row = lax.broadcasted_iota(jnp.int32, cu.shape, 0)
    hp = prev_ref[HALO - 1:HALO, :].astype(f32) * has_prev
    hn = next_ref[0:1, :].astype(f32) * has_next
    up = jnp.where(row == 0, hp, pltpu.roll(cu, 1, 0))
    dn = jnp.where(row == TM - 1, hn, pltpu.roll(cu, TM - 1, 0))
    cw = cw_ref[...]
    z = cw[0:1, :] * up + cw[1:2, :] * cu + cw[2:3, :] * dn
    yb = (bg_ref[...].astype(f32) * z).astype(bf16)
    y = jnp.dot(yb, wo_ref[...], preferred_element_type=f32)
    _route_epilogue(x_ref[...], y, *rest)


def _mixer_tile_spec(n, last):
    return pl.BlockSpec((TM, n), lambda i: (jnp.minimum(i, last), 0))


def _route_specs(d, tiles_per_b, n_batch):
    last = tiles_per_b * n_batch - 1

    def routed(i):
        return jnp.maximum(i - 1, 0)

    ins = [_mod_spec(2, tiles_per_b, n_batch), _row_spec(d), _mod_spec(3, tiles_per_b, n_batch),
           _mod_spec(4, tiles_per_b, n_batch), _full_spec((2, d, LANES)), _row_spec(LANES),
           _full_spec((TM, TM)), _full_spec((LANES, LANES))]
    outs = [_mixer_tile_spec(d, last), _mixer_tile_spec(d, last),
            pl.BlockSpec((None, META_ROWS, TM), lambda i: (routed(i), 0, 0)),
            pl.BlockSpec((TM, 8), lambda i: (routed(i), 0)),
            pl.BlockSpec((None, SUBLANES, LANES), lambda i: (routed(i), 0, 0)), _row_spec(LANES)]
    return ins, outs


def _route_out_shapes(t, d):
    return [jax.ShapeDtypeStruct((t, d), f32), jax.ShapeDtypeStruct((t, d), bf16),
            jax.ShapeDtypeStruct((t // TM, META_ROWS, TM), jnp.int32), jax.ShapeDtypeStruct((t, 8), f32),
            jax.ShapeDtypeStruct((t // TM, SUBLANES, LANES), jnp.int32),
            jax.ShapeDtypeStruct((1, LANES), f32)]


def _rows_to_tiles(tref, x):
    n = x.shape[0]
    for s in range(SUBLANES):
        tref[pl.ds(s, n, stride=SUBLANES), :] = x[:, s * LANES:(s + 1) * LANES]


def _tiles_to_rows(tref):
    n = tref.shape[0] // SUBLANES
    return jnp.concatenate([tref[pl.ds(s, n, stride=SUBLANES), :] for s in range(SUBLANES)], axis=-1)


def _tile_rows(ref, row, n):
    return ref.at[pl.ds(row * SUBLANES, n * SUBLANES)]


def _run_copies(chunks_ref, copy):
    n_chunks = chunks_ref[1, 0]

    def chunk(q, carry):
        copy(q * RUN_ALIGN, chunks_ref[0, q]).start()
        return carry

    lax.fori_loop(0, n_chunks, chunk, 0)
    return n_chunks


def _wait_chunks(n_chunks, copy):
    def wait_rows(rows):
        def wait_one(c, carry):
            copy(0, 0, rows).wait()
            return carry
        return wait_one

    lax.fori_loop(0, n_chunks // WAIT_GROUP, wait_rows(WAIT_GROUP * RUN_ALIGN), 0)
    lax.fori_loop(0, n_chunks % WAIT_GROUP, wait_rows(RUN_ALIGN), 0)


def _dispatch_body(seg_ref, chunks_ref, meta_ref, f_ref, xp_ref, sbuf, zbuf, pending, sem, zsem):
    i = pl.program_id(0)
    n_blocks = xp_ref.shape[0] // (EXPERT_BLOCK * SUBLANES)

    @pl.when(i == 0)
    def _():
        zbuf[...] = jnp.zeros_like(zbuf)

        def pad_fill(e, wait):
            n_pad = seg_ref[2 * MOE_EXPERTS + e]
            off = seg_ref[MOE_EXPERTS + e]
            for piece in PAD_PIECES:
                take = (n_pad & piece) != 0
                cp = pltpu.make_async_copy(_tile_rows(zbuf, 0, piece), _tile_rows(xp_ref, off, piece), zsem)
                pl.when(take)(cp.wait if wait else cp.start)
                off = off + jnp.where(take, piece, 0)

        def tail_fill(b, wait):
            cp = pltpu.make_async_copy(zbuf, _tile_rows(xp_ref, b * EXPERT_BLOCK, EXPERT_BLOCK), zsem)
            cp.wait() if wait else cp.start()

        for wait in (False, True):
            lax.fori_loop(0, MOE_EXPERTS, lambda e, c, w=wait: (pad_fill(e, w), c)[1], 0)
            lax.fori_loop(seg_ref[3 * MOE_EXPERTS], n_blocks, lambda b, c, w=wait: (tail_fill(b, w), c)[1], 0)

        pending[0] = 0

    buf = sbuf.at[i % 2]
    pos = lax.broadcasted_iota(jnp.int32, (LOCAL_ROWS, TM), 0)
    m = meta_ref[...]
    perm = jnp.where((pos == m[0:1, :]) | (pos == m[1:2, :]), 1.0, 0.0).astype(bf16)
    _rows_to_tiles(buf, jnp.dot(perm, f_ref[...], preferred_element_type=f32))

    def copy(local_row, slot_row, rows=RUN_ALIGN):
        return pltpu.make_async_copy(_tile_rows(buf, local_row, rows), _tile_rows(xp_ref, slot_row, rows), sem)

    _wait_chunks(pending[0], copy)
    pending[0] = _run_copies(chunks_ref, copy)

    @pl.when(i == pl.num_programs(0) - 1)
    def _():
        _wait_chunks(pending[0], copy)


def _dispatch_call(seg, chunks, meta, f, n_rows):
    t, d = f.shape
    grid_spec = pltpu.PrefetchScalarGridSpec(
        num_scalar_prefetch=1,
        grid=(t // TM,),
        in_specs=[pl.BlockSpec((None, SUBLANES, LANES), lambda i, seg: (i, 0, 0), memory_space=pltpu.SMEM),
                  pl.BlockSpec((None, META_ROWS, TM), lambda i, seg: (i, 0, 0)),
                  pl.BlockSpec((TM, d), lambda i, seg: (i, 0))],
        out_specs=pl.BlockSpec(memory_space=pl.ANY),
        scratch_shapes=[pltpu.VMEM((2, LOCAL_ROWS * SUBLANES, LANES), f32),
                        pltpu.VMEM((EXPERT_BLOCK * SUBLANES, LANES), f32),
                        pltpu.SMEM((1,), jnp.int32),
                        pltpu.SemaphoreType.DMA(()), pltpu.SemaphoreType.DMA(())],
    )
    return pl.pallas_call(
        _dispatch_body,
        grid_spec=grid_spec,
        out_shape=jax.ShapeDtypeStruct((n_rows * SUBLANES, LANES), f32),
        compiler_params=_params("arbitrary"),
        name="moe_dispatch",
    )(seg, chunks, meta, f)


def _expert_body(be_ref, na_ref, x_ref, wg_hbm, wu_hbm, wd_hbm, o_ref,
                 wg_f, wu_f, wd_f, wg_s, wu_s, wd_s, wsem, *, layer):
    i = pl.program_id(0)
    active = i < na_ref[0]
    e = be_ref[i]
    changed = (i == 0) | (e != be_ref[jnp.maximum(i - 1, 0)])

    def fetch(expert):
        pairs = ((wg_hbm, wg_f), (wu_hbm, wu_f), (wd_hbm, wd_f))
        return [pltpu.make_async_copy(w.at[layer, expert], buf, wsem.at[j]) for j, (w, buf) in enumerate(pairs)]

    @pl.when(i == 0)
    def _():
        for cp in fetch(e):
            cp.start()

    @pl.when(active & changed)
    def _():
        for cp in fetch(e):
            cp.wait()
        wg_s[...] = wg_f[...].astype(bf16)
        wu_s[...] = wu_f[...].astype(bf16)
        wd_s[...] = wd_f[...].astype(bf16)

        @pl.when(e + 1 < MOE_EXPERTS)
        def _():
            for cp in fetch(e + 1):
                cp.start()

    @pl.when(active)
    def _():
        x = _tiles_to_rows(x_ref).astype(bf16)
        g = jnp.dot(x, wg_s[...], preferred_element_type=f32)
        u = jnp.dot(x, wu_s[...], preferred_element_type=f32)
        h = (_silu(g) * u).astype(bf16)
        _rows_to_tiles(o_ref, jnp.dot(h, wd_s[...], preferred_element_type=f32))

    @pl.when(jnp.logical_not(active))
    def _():
        o_ref[...] = jnp.zeros_like(o_ref)


def _expert_call(blk_e, n_act, xp, w_gate, w_up, w_down, layer):
    n_rows = xp.shape[0] // SUBLANES
    d, ff = w_gate.shape[-2:]
    blk_rows = EXPERT_BLOCK * SUBLANES

    def blk(i, be, na):
        return jnp.maximum(jnp.minimum(i, na[0] - 1), 0)

    any_spec = pl.BlockSpec(memory_space=pl.ANY)
    grid_spec = pltpu.PrefetchScalarGridSpec(
        num_scalar_prefetch=2,
        grid=(n_rows // EXPERT_BLOCK,),
        in_specs=[pl.BlockSpec((blk_rows, LANES), lambda i, be, na: (blk(i, be, na), 0)),
                  any_spec, any_spec, any_spec],
        out_specs=pl.BlockSpec((blk_rows, LANES), lambda i, be, na: (i, 0)),
        scratch_shapes=[pltpu.VMEM((d, ff), f32), pltpu.VMEM((d, ff), f32), pltpu.VMEM((ff, d), f32),
                        pltpu.VMEM((d, ff), bf16), pltpu.VMEM((d, ff), bf16), pltpu.VMEM((ff, d), bf16),
                        pltpu.SemaphoreType.DMA((3,))],
    )
    return pl.pallas_call(
        functools.partial(_expert_body, layer=layer),
        grid_spec=grid_spec,
        out_shape=jax.ShapeDtypeStruct((n_rows * SUBLANES, LANES), f32),
        compiler_params=_params("arbitrary"),
        name="moe_experts",
    )(blk_e, n_act, xp, w_gate, w_up, w_down)


def _combine_body(chunks_ref, next_chunks_ref, x_ref, wts_ref, g2_ref, fg_ref, yp_ref, o_ref,
                  gbuf, pending, sem, *, final):
    step = pl.program_id(0) * pl.num_programs(1) + pl.program_id(1)
    n_steps = pl.num_programs(0) * pl.num_programs(1)

    def fetch(buf):
        def copy(local_row, slot_row, rows=RUN_ALIGN):
            return pltpu.make_async_copy(_tile_rows(yp_ref, slot_row, rows),
                                         _tile_rows(gbuf.at[buf], local_row, rows), sem.at[buf])
        return copy

    @pl.when(step == 0)
    def _():
        gbuf[...] = jnp.zeros_like(gbuf)
        pending[0] = _run_copies(chunks_ref, fetch(0))

    def run(cur):
        @pl.when(step + 1 < n_steps)
        def _():
            pending[1 - cur] = _run_copies(next_chunks_ref, fetch(1 - cur))

        _wait_chunks(pending[cur], fetch(cur))
        ys = _tiles_to_rows(gbuf.at[cur]).astype(bf16)
        w = wts_ref[...]
        pos = lax.broadcasted_iota(jnp.int32, (TM, LOCAL_ROWS), 1).astype(f32)
        y = None
        for k in range(MOE_TOP_K):
            pick = jnp.where(pos == w[:, MOE_TOP_K + k:MOE_TOP_K + k + 1], 1.0, 0.0).astype(bf16)
            yk = w[:, k:k + 1] * jnp.dot(pick, ys, preferred_element_type=f32)
            y = yk if y is None else y + yk
        xn = x_ref[...] + g2_ref[...] * y
        if final:
            xn = _rms(xn) * fg_ref[...]
        o_ref[...] = xn

    for cur in range(2):
        pl.when(step % 2 == cur)(functools.partial(run, cur))


def _combine_call(chunks, x, wts, mods3, final_g, yp, n_batch, tiles_per_b, final):
    t, d = x.shape
    t0 = 1 if final else 0
    nt = tiles_per_b - t0

    def tile(b, j):
        return b * tiles_per_b + t0 + j

    def next_tile(b, j):
        nxt = jnp.minimum(b * nt + j + 1, n_batch * nt - 1)
        return tile(nxt // nt, nxt % nt)

    def g2_map(b, j):
        r = jnp.where(t0 + j == 0, n_batch, b)
        return (r * N_MOD + 5, 0, 0)

    return pl.pallas_call(
        functools.partial(_combine_body, final=final),
        grid=(n_batch, nt),
        in_specs=[pl.BlockSpec((None, SUBLANES, LANES), lambda b, j: (tile(b, j), 0, 0),
                               memory_space=pltpu.SMEM),
                  pl.BlockSpec((None, SUBLANES, LANES), lambda b, j: (next_tile(b, j), 0, 0),
                               memory_space=pltpu.SMEM),
                  pl.BlockSpec((TM, d), lambda b, j: (tile(b, j), 0)),
                  pl.BlockSpec((TM, 8), lambda b, j: (tile(b, j), 0)),
                  pl.BlockSpec((None, 1, d), g2_map),
                  pl.BlockSpec((1, d), lambda b, j: (0, 0)),
                  pl.BlockSpec(memory_space=pl.ANY)],
        out_specs=pl.BlockSpec((TM, d), lambda b, j: (b * nt + j, 0)),
        scratch_shapes=[pltpu.VMEM((2, LOCAL_ROWS * SUBLANES, LANES), f32), pltpu.SMEM((2,), jnp.int32),
                        pltpu.SemaphoreType.DMA((2,))],
        out_shape=jax.ShapeDtypeStruct((n_batch * nt * TM, d), f32),
        compiler_params=_params("arbitrary", "arbitrary"),
        name="moe_combine",
    )(chunks, chunks, x, wts, mods3, final_g, yp)


def _chunk_table(runs, pstarts):
    n_chunk_lanes = LOCAL_ROWS // RUN_ALIGN
    assert n_chunk_lanes <= LANES
    experts = slice(ROUTE_LANE0, ROUTE_LANE0 + MOE_EXPERTS)
    n, base, lo = runs[:, 0, experts], runs[:, 1, experts], runs[:, 2, experts]
    ends = lo + (n + RUN_ALIGN - 1) // RUN_ALIGN * RUN_ALIGN
    row0 = jnp.arange(LANES, dtype=jnp.int32) * RUN_ALIGN
    owner = jnp.sum((ends[:, None, :] <= row0[None, :, None]).astype(jnp.int32), axis=-1)
    shift = pstarts[None, :] + base - lo
    mine = owner[:, :, None] == jnp.arange(MOE_EXPERTS, dtype=jnp.int32)[None, None, :]
    slot = row0[None, :] + jnp.sum(jnp.where(mine, shift[:, None, :], 0), axis=-1)
    count = jnp.broadcast_to(ends[:, -1:] // RUN_ALIGN, slot.shape)
    rest = jnp.zeros((runs.shape[0], SUBLANES - 2, LANES), jnp.int32)
    return jnp.concatenate([slot[:, None, :], count[:, None, :], rest], axis=1).astype(jnp.int32)
def _rope_tables(seq, n_ctx, half):
    inv = ROPE_THETA ** (-jnp.arange(half, dtype=f32) / half)
    t = jnp.arange(seq, dtype=jnp.int32)
    ang_r = (t // GRID_W).astype(f32)[:, None] * inv[None, :]
    ang_c = (t % GRID_W).astype(f32)[:, None] * inv[None, :]
    gap = jnp.zeros((seq, LANES // 2 - 2 * half), f32)
    cos_h = jnp.concatenate([jnp.cos(ang_r), jnp.cos(ang_c), gap], axis=-1)
    sin_h = jnp.concatenate([jnp.sin(ang_r), jnp.sin(ang_c), gap], axis=-1)
    cos = jnp.concatenate([cos_h, cos_h], axis=-1)
    sin = jnp.concatenate([-sin_h, sin_h], axis=-1)
    ctx_c = jnp.broadcast_to((cos[:1] != 0.0).astype(f32), (n_ctx, LANES))
    ctx_s = jnp.zeros((n_ctx, LANES), f32)
    return jnp.concatenate([ctx_c, cos], axis=0), jnp.concatenate([ctx_s, sin], axis=0)


def _rope_lanes(w, half):
    lead = w.shape[:-1]
    halves = w.reshape(lead + (2, 2, half)).swapaxes(-3, -2).reshape(lead + (2, 2 * half))
    pad = [(0, 0)] * (len(lead) + 1) + [(0, LANES // 2 - 2 * half)]
    return jnp.pad(halves, pad).reshape(lead + (LANES,))


def _table_spec(tiles_per_b):
    return pl.BlockSpec((TM, LANES), lambda i: (i % tiles_per_b, 0))


def kernel(x, c, ctx, c_ctx, w_mod, b_mod, norm_mix_g, norm_ffn_g, mla_w_dq, mla_g_q, mla_w_uq, mla_w_dkv, mla_g_kv, mla_w_ukv, mla_w_o, conv_w_in, conv_w, conv_w_out, gqa_w_qkv, gqa_q_norm_g, gqa_k_norm_g, gqa_w_o, moe_w_group, moe_b_group, moe_w_expert, moe_b_expert, moe_w_gate, moe_w_up, moe_w_down, final_norm_g):
    n_batch, seq, d = x.shape
    n_ctx = ctx.shape[1]
    depth = w_mod.shape[0]
    assert n_ctx == TM and seq % TM == 0 and d == SUBLANES * LANES
    nt = n_ctx + seq
    tiles_per_b = nt // TM
    t_all = n_batch * nt
    n_tiles = t_all // TM
    n_assign = t_all * MOE_TOP_K
    n_blocks = -(-(n_assign + MOE_EXPERTS * (RUN_ALIGN - 1)) // EXPERT_BLOCK) + MOE_EXPERTS
    n_rows = n_blocks * EXPERT_BLOCK

    xs = jnp.concatenate([ctx, x], axis=1).reshape(t_all, d)
    cond8 = jnp.concatenate([c, c_ctx[None], jnp.zeros((8 - n_batch - 1, d), f32)], axis=0)
    mods = _mod_call(cond8, w_mod, b_mod)
    mods = mods.reshape(depth, 8 * N_MOD, 1, d)

    mla_tabs = _rope_tables(seq, n_ctx, MLA_ROPE // 4)
    gqa_tabs = _rope_tables(seq, n_ctx, GQA_HEAD_DIM // 4)
    ltri = jnp.tril(jnp.ones((TM, TM), f32), -1).astype(bf16)
    upper = jnp.triu(jnp.ones((LANES, LANES), f32), 1)
    mspec = functools.partial(_mod_spec, tiles_per_b=tiles_per_b, n_batch=n_batch)
    tab_specs = [_table_spec(tiles_per_b)] * 2
    route_in, route_out = _route_specs(d, tiles_per_b, n_batch)
    route_shapes = _route_out_shapes(t_all, d)
    route_scratch = [pltpu.VMEM((1, LANES), f32), pltpu.VMEM((2, TM, LANES), f32)]
    mtile = functools.partial(_mixer_tile_spec, last=n_tiles - 1)

    for i in range(depth):
        kind, j = i % N_MIXERS, i // N_MIXERS
        m3 = mods[i]
        gmix = norm_mix_g[i][None]
        head_in = [_tile_spec(d), mspec(0), mspec(1), _row_spec(d)]
        wr = jnp.pad(jnp.concatenate([moe_w_group[i], moe_w_expert[i]], axis=1),
                     ((0, 0), (0, LANES - MOE_GROUPS - MOE_EXPERTS)))
        wr_hi = wr.astype(bf16)
        wr = jnp.stack([wr_hi, (wr - wr_hi.astype(f32)).astype(bf16)])
        br = jnp.pad(jnp.concatenate([moe_b_group[i], moe_b_expert[i]]),
                     (0, LANES - MOE_GROUPS - MOE_EXPERTS))[None]
        route_args = (m3, norm_ffn_g[i][None], m3, m3, wr, br, ltri, upper)

        if kind == 0:
            dk, dv = 2 * LANES, MLA_V
            half = MLA_ROPE // 4
            wcat = jnp.concatenate([mla_w_dq[j], mla_w_dkv[j][:, :MLA_KV_LORA],
                                    _rope_lanes(mla_w_dkv[j][:, MLA_KV_LORA:], half)], axis=1).astype(bf16)
            wuq3 = mla_w_uq[j].reshape(MLA_Q_LORA, MLA_HEADS, MLA_NOPE + MLA_ROPE)
            wuq = jnp.concatenate([wuq3[..., :MLA_NOPE], _rope_lanes(wuq3[..., MLA_NOPE:], half)], axis=-1
                                  ).reshape(MLA_Q_LORA, MLA_HEADS * dk).astype(bf16)
            wukv3 = mla_w_ukv[j].reshape(MLA_KV_LORA, MLA_HEADS, MLA_NOPE + MLA_V)
            wukv = jnp.concatenate([wukv3[..., :MLA_NOPE].reshape(MLA_KV_LORA, -1),
                                    wukv3[..., MLA_NOPE:].reshape(MLA_KV_LORA, -1)], axis=1).astype(bf16)
            q, k, v = pl.pallas_call(
                functools.partial(_mla_proj_body, scale=(MLA_NOPE + MLA_ROPE) ** -0.5 * LOG2E),
                grid=(n_tiles,),
                in_specs=head_in + [_full_spec(wcat.shape), _row_spec(MLA_Q_LORA), _row_spec(MLA_KV_LORA),
                                    _full_spec(wuq.shape), _full_spec(wukv.shape)] + tab_specs,
                out_specs=[_tile_spec(MLA_HEADS * dk), _tile_spec(MLA_HEADS * dk), _tile_spec(MLA_HEADS * dv)],
                out_shape=[jax.ShapeDtypeStruct((t_all, MLA_HEADS * dk), bf16),
                           jax.ShapeDtypeStruct((t_all, MLA_HEADS * dk), bf16),
                           jax.ShapeDtypeStruct((t_all, MLA_HEADS * dv), bf16)],
                compiler_params=_params("arbitrary"),
                name="mla_proj",
            )(xs, m3, m3, gmix, wcat, mla_g_q[j][None], mla_g_kv[j][None], wuq, wukv, *mla_tabs)
            att = _attn_call(q, k, v, n_batch, MLA_HEADS, MLA_HEADS, dk, dv, n_ctx)
            w_o = mla_w_o[j].astype(bf16)
        elif kind == 2:
            dk = dv = GQA_HEAD_DIM
            half = GQA_HEAD_DIM // 4
            n_qk = GQA_HEADS + GQA_KV_HEADS
            wqk = _rope_lanes(gqa_w_qkv[j][:, :n_qk * dk].reshape(d, n_qk, dk), half).reshape(d, n_qk * dk)
            wqkv = jnp.concatenate([wqk, gqa_w_qkv[j][:, n_qk * dk:]], axis=1).astype(bf16)
            q, k, v = pl.pallas_call(
                functools.partial(_gqa_proj_body, scale=GQA_HEAD_DIM ** -0.5 * LOG2E),
                grid=(n_tiles,),
                in_specs=head_in + [_full_spec(wqkv.shape), _row_spec(dk), _row_spec(dk)] + tab_specs,
                out_specs=[_tile_spec(GQA_HEADS * dk), _tile_spec(GQA_KV_HEADS * dk),
                           _tile_spec(GQA_KV_HEADS * dv)],
                out_shape=[jax.ShapeDtypeStruct((t_all, GQA_HEADS * dk), bf16),
                           jax.ShapeDtypeStruct((t_all, GQA_KV_HEADS * dk), bf16),
                           jax.ShapeDtypeStruct((t_all, GQA_KV_HEADS * dv), bf16)],
                compiler_params=_params("arbitrary"),
                name="gqa_proj",
            )(xs, m3, m3, gmix, wqkv, _rope_lanes(gqa_q_norm_g[j][None], half),
              _rope_lanes(gqa_k_norm_g[j][None], half), *gqa_tabs)
            att = _attn_call(q, k, v, n_batch, GQA_HEADS, GQA_KV_HEADS, dk, dv, n_ctx)
            w_o = gqa_w_o[j].astype(bf16)

        if kind == 1:
            w_in = conv_w_in[j].astype(bf16)
            bg, cu = pl.pallas_call(
                _conv_proj_body,
                grid=(n_tiles,),
                in_specs=head_in + [_full_spec(w_in.shape)],
                out_specs=[_tile_spec(d), _tile_spec(d)],
                out_shape=[jax.ShapeDtypeStruct((t_all, d), bf16)] * 2,
                compiler_params=_params("arbitrary"),
                name="conv_proj",
            )(xs, m3, m3, gmix, w_in)
            w_o = conv_w_out[j].astype(bf16)
            per = TM // HALO
            last = t_all // HALO - 1
            xs, f, meta, wts, runs, cnt = pl.pallas_call(
                functools.partial(_conv_out_body, tiles_per_b=tiles_per_b),
                grid=(n_tiles + 1,),
                in_specs=[mtile(d),
                          pl.BlockSpec((HALO, d), lambda i: (
                              jnp.maximum(jnp.minimum(i, n_tiles - 1) * per - 1, 0), 0)),
                          pl.BlockSpec((HALO, d), lambda i: (jnp.minimum((i + 1) * per, last), 0)),
                          mtile(d), _full_spec((3, d)), _full_spec(w_o.shape), mtile(d)] + route_in,
                out_specs=route_out,
                out_shape=route_shapes,
                scratch_shapes=route_scratch,
                compiler_params=_params("arbitrary"),
                name="conv_out_route",
            )(cu, cu, cu, bg, conv_w[j], w_o, xs, *route_args)
        else:
            xs, f, meta, wts, runs, cnt = pl.pallas_call(
                _attn_out_body,
                grid=(n_tiles + 1,),
                in_specs=[mtile(att.shape[-1]), _full_spec(w_o.shape), mtile(d)] + route_in,
                out_specs=route_out,
                out_shape=route_shapes,
                scratch_shapes=route_scratch,
                compiler_params=_params("arbitrary"),
                name="attn_out_route",
            )(att, w_o, xs, *route_args)

        counts = cnt[0, ROUTE_LANE0:ROUTE_LANE0 + MOE_EXPERTS].astype(jnp.int32)
        padded = (counts + RUN_ALIGN - 1 + EXPERT_BLOCK - 1) // EXPERT_BLOCK * EXPERT_BLOCK
        pends = jnp.cumsum(padded)
        pstarts = pends - padded
        blk_row0 = jnp.arange(n_blocks, dtype=jnp.int32) * EXPERT_BLOCK
        blk_e = jnp.minimum(jnp.sum((pends[None, :] <= blk_row0[:, None]).astype(jnp.int32), axis=1),
                            MOE_EXPERTS - 1)
        n_act = (pends[-1:] // EXPERT_BLOCK).astype(jnp.int32)
        seg = jnp.concatenate([pstarts, pstarts + counts, padded - counts, n_act,
                               jnp.zeros((MOE_EXPERTS - 1,), jnp.int32)]).astype(jnp.int32)

        chunks = _chunk_table(runs, pstarts)

        xp = _dispatch_call(seg, chunks, meta, f, n_rows)
        yp = _expert_call(blk_e, n_act, xp, moe_w_gate, moe_w_up, moe_w_down, i)
        final = i == depth - 1
        xs = _combine_call(chunks, xs, wts, m3, final_norm_g[None], yp, n_batch, tiles_per_b, final)

    return xs.reshape(n_batch, seq, d)
```
